```python
import jax, jax.numpy as jnp
from jax import lax
import numpy as np

D_MODEL = 4096
BATCH = 8
SEQ = 2048
DEPTH = 2
DEC_BATCH = 32
DEC_SEQ = 32
PAST_LEN = 1024

CHUNK = 64
Q_BLOCK = 128
EPS = 1e-6
CONV_WIDTH = 1024
CONV_K = 31
CONV_STATE = CONV_K - 1
M_HEADS = 4
M_DK = 128
M_DV = 256
M_WIDTH = M_HEADS * M_DV
FORGET_BIAS = 3.0
A_HEADS = 16
NOPE_DIM = 128
ROPE_DIM = 64
V_DIM = 128
Q_LORA = 1024
KV_LORA = 512
A_WIDTH = A_HEADS * V_DIM
ROPE_THETA = 10000.0
ATTN_SCALE = (NOPE_DIM + ROPE_DIM) ** -0.5
N_BRANCH = 3

IN_SIZES = (CONV_WIDTH, CONV_WIDTH, CONV_WIDTH,
            M_HEADS * M_DK, M_HEADS * M_DK, M_WIDTH,
            M_HEADS, M_HEADS, M_WIDTH, M_WIDTH,
            Q_LORA, KV_LORA, ROPE_DIM, A_WIDTH,
            N_BRANCH * D_MODEL)
N_IN = sum(IN_SIZES)
SPLIT_AT = tuple(int(s) for s in np.cumsum(IN_SIZES)[:-1])
F_OFFSET = sum(IN_SIZES[:7])

kernel_name = "hybrid_conv_mlstm_mla_stream_step"


def rmsnorm(x, g):
    xf = x.astype(jnp.float32)
    y = xf * lax.rsqrt(jnp.mean(xf * xf, axis=-1, keepdims=True) + EPS)
    return (y * g.astype(jnp.float32)).astype(x.dtype)


def layernorm(x, g, b):
    xf = x.astype(jnp.float32)
    mu = jnp.mean(xf, axis=-1, keepdims=True)
    var = jnp.mean(jnp.square(xf - mu), axis=-1, keepdims=True)
    y = (xf - mu) * lax.rsqrt(var + EPS) * g.astype(jnp.float32) + b.astype(jnp.float32)
    return y.astype(x.dtype)


def rope(x, pos):
    half = ROPE_DIM // 2
    freqs = ROPE_THETA ** (-jnp.arange(half, dtype=jnp.float32) / half)
    ang = pos.astype(jnp.float32)[:, None] * freqs
    shape = (1, pos.shape[0]) + (1,) * (x.ndim - 3) + (half,)
    cos = jnp.cos(ang).reshape(shape)
    sin = jnp.sin(ang).reshape(shape)
    xf = x.astype(jnp.float32)
    x1, x2 = xf[..., :half], xf[..., half:]
    return jnp.concatenate([x1 * cos - x2 * sin, x2 * cos + x1 * sin], axis=-1).astype(x.dtype)


def causal_dwconv(u, prev, w, b):
    full = jnp.concatenate([prev.astype(u.dtype), u], axis=1)
    out = lax.conv_general_dilated(full, w[:, None, :].astype(u.dtype), window_strides=(1,), padding='VALID',
                                   dimension_numbers=('NWC', 'WIO', 'NWC'), feature_group_count=u.shape[-1])
    return out + b, full[:, -CONV_STATE:]


def mlstm_chunk(carry, inp):
    C, n, m = carry
    q, k, v, li, lf = inp
    L = q.shape[1]
    bt = jnp.cumsum(lf, axis=1).transpose(0, 2, 1)
    lit = li.transpose(0, 2, 1)
    causal = jnp.tril(jnp.ones((L, L), dtype=bool))
    logw = jnp.where(causal, bt[..., :, None] - bt[..., None, :] + lit[..., None, :], -jnp.inf)
    a = bt + m[..., None]
    m_t = jnp.maximum(a, jnp.max(logw, axis=-1))
    w_inter = jnp.exp(a - m_t)
    sw = jnp.exp(logw - m_t[..., None]) * jnp.einsum('blhk,bshk->bhls', q, k)
    num = (jnp.einsum('bhls,bshv->blhv', sw, v)
           + jnp.einsum('blhk,bhkv->blhv', q, C) * w_inter.transpose(0, 2, 1)[..., None])
    den = jnp.sum(sw, axis=-1) + w_inter * jnp.einsum('blhk,bhk->bhl', q, n)
    denom = jnp.maximum(jnp.abs(den), jnp.exp(-m_t)).transpose(0, 2, 1)[..., None]
    h = num / denom
    b_last = bt[..., -1]
    gs = b_last[..., None] - bt + lit
    m_new = jnp.maximum(b_last + m, jnp.max(gs, axis=-1))
    ws = jnp.exp(gs - m_new[..., None])
    decay = jnp.exp(b_last + m - m_new)
    C_new = decay[..., None, None] * C + jnp.einsum('bhs,bshk,bshv->bhkv', ws, k, v)
    n_new = decay[..., None] * n + jnp.einsum('bhs,bshk->bhk', ws, k)
    return (C_new, n_new, m_new), h


def mlstm(q, k, v, li, lf, C, n, m):
    B, L = q.shape[:2]
    if L <= CHUNK:
        (C, n, m), h = mlstm_chunk((C, n, m), (q, k, v, li, lf))
        return h, C, n, m
    nc = L // CHUNK

    def to_chunks(t):
        return jnp.moveaxis(t.reshape((B, nc, CHUNK) + t.shape[2:]), 1, 0)

    (C, n, m), hs = lax.scan(mlstm_chunk, (C, n, m), (to_chunks(q), to_chunks(k), to_chunks(v), to_chunks(li), to_chunks(lf)))
    h = jnp.moveaxis(hs, 0, 1).reshape((B, L) + hs.shape[3:])
    return h, C, n, m


def chunk_attention(qn, qr, qpos, kn, kr, v, kpos):
    s = (jnp.einsum('bqhd,bkhd->bhqk', qn, kn) + jnp.einsum('bqhd,bkd->bhqk', qr, kr)).astype(jnp.float32) * ATTN_SCALE
    mask = (kpos[None, :] // CHUNK) <= (qpos[:, None] // CHUNK)
    s = jnp.where(mask, s, -jnp.inf)
    p = jax.nn.softmax(s, axis=-1).astype(v.dtype)
    return jnp.einsum('bhqk,bkhd->bqhd', p, v)


def mla_attend(qn, qr, qpos, kn, kr, v, kpos):
    B, L = qn.shape[:2]
    if L <= Q_BLOCK or L % Q_BLOCK:
        return chunk_attention(qn, qr, qpos, kn, kr, v, kpos)
    nb = L // Q_BLOCK

    def blocks(t):
        return jnp.moveaxis(t.reshape((B, nb, Q_BLOCK) + t.shape[2:]), 1, 0)

    out = lax.map(lambda a: chunk_attention(a[0], a[1], a[2], kn, kr, v, kpos),
                  (blocks(qn), blocks(qr), qpos.reshape(nb, Q_BLOCK)))
    return jnp.moveaxis(out, 0, 1).reshape((B, L) + out.shape[3:])


def mixer_layer(x, pos, conv_prev, mC, mn, mm, lat_past, kr_past,
                ln_g, w_in, b_in, conv_w, conv_b, conv_ln_g, conv_ln_b, w_pc, m_norm_g, w_pm,
                cq_g, ckv_g, qn_g, qr_g, kn_g, kr_g, w_uq, w_ukv, w_pa, w_out):
    B, L, _ = x.shape
    h = rmsnorm(x, ln_g)
    proj = h @ w_in + b_in
    (c_a, c_b, c_z, m_q, m_k, m_v, m_i, m_f, m_o, m_z,
     a_cq, a_ckv, a_kr, a_z, gates) = jnp.split(proj, SPLIT_AT, axis=-1)

    u = c_a * jax.nn.sigmoid(c_b)
    cv, conv_new = causal_dwconv(u, conv_prev, conv_w, conv_b)
    cv = jax.nn.silu(layernorm(cv, conv_ln_g, conv_ln_b)) * jax.nn.silu(c_z)
    y_c = cv @ w_pc

    q = m_q.reshape(B, L, M_HEADS, M_DK).astype(jnp.float32)
    k = m_k.reshape(B, L, M_HEADS, M_DK).astype(jnp.float32) * (M_DK ** -0.5)
    v = m_v.reshape(B, L, M_HEADS, M_DV).astype(jnp.float32)
    li = m_i.astype(jnp.float32)
    lf = jax.nn.log_sigmoid(m_f.astype(jnp.float32))
    hm, mC, mn, mm = mlstm(q, k, v, li, lf, mC, mn, mm)
    hm = rmsnorm(hm, m_norm_g.reshape(M_HEADS, M_DV)).reshape(B, L, M_WIDTH).astype(x.dtype)
    hm = hm * jax.nn.sigmoid(m_o) * jax.nn.silu(m_z)
    y_m = hm @ w_pm

    qa = (rmsnorm(a_cq, cq_g) @ w_uq).reshape(B, L, A_HEADS, NOPE_DIM + ROPE_DIM)
    q_nope = rmsnorm(qa[..., :NOPE_DIM], qn_g)
    q_rope = rope(rmsnorm(qa[..., NOPE_DIM:], qr_g), pos)
    ckv = rmsnorm(a_ckv, ckv_g)
    kr_new = rope(rmsnorm(a_kr, kr_g), pos)
    if lat_past is None:
        lat_all, kr_all, kpos = ckv, kr_new, pos
    else:
        lat_all = jnp.concatenate([lat_past.astype(ckv.dtype), ckv], axis=1)
        kr_all = jnp.concatenate([kr_past.astype(kr_new.dtype), kr_new], axis=1)
        kpos = jnp.arange(lat_all.shape[1], dtype=jnp.int32)
    T = lat_all.shape[1]
    kv = (lat_all @ w_ukv).reshape(B, T, A_HEADS, NOPE_DIM + V_DIM)
    k_nope = rmsnorm(kv[..., :NOPE_DIM], kn_g)
    va = kv[..., NOPE_DIM:]
    ao = mla_attend(q_nope, q_rope, pos, k_nope, kr_all, va, kpos).reshape(B, L, A_WIDTH)
    y_a = (ao * jax.nn.silu(a_z)) @ w_pa

    g_c, g_m, g_a = jnp.split(jax.nn.sigmoid(gates), N_BRANCH, axis=-1)
    y = (g_c * y_c + g_m * y_m + g_a * y_a) @ w_out
    return x + y, conv_new, mC, mn, mm, ckv, kr_new


def setup_inputs(seed: int = 0) -> dict:
    key = jax.random.key(seed)
    ks = jax.random.split(key, 32)

    def nrm(k, shape, scale):
        return jax.random.normal(k, shape, jnp.float32) * scale

    b_in = nrm(ks[10], (DEPTH, N_IN), 0.01).at[:, F_OFFSET:F_OFFSET + M_HEADS].add(FORGET_BIAS)
    return {
        "x_prompt": nrm(ks[0], (BATCH, SEQ, D_MODEL), 1.0),
        "x_sample": nrm(ks[1], (DEC_BATCH, DEC_SEQ, D_MODEL), 1.0),
        "cache_kv_latent": nrm(ks[2], (DEPTH, DEC_BATCH, PAST_LEN, KV_LORA), 1.0),
        "cache_k_rope": nrm(ks[3], (DEPTH, DEC_BATCH, PAST_LEN, ROPE_DIM), 1.0),
        "state_conv": nrm(ks[4], (DEPTH, DEC_BATCH, CONV_STATE, CONV_WIDTH), 0.5),
        "state_mlstm_C": nrm(ks[5], (DEPTH, DEC_BATCH, M_HEADS, M_DK, M_DV), 0.1),
        "state_mlstm_n": nrm(ks[6], (DEPTH, DEC_BATCH, M_HEADS, M_DK), 0.1),
        "state_mlstm_m": nrm(ks[7], (DEPTH, DEC_BATCH, M_HEADS), 1.0),
        "ln_g": 1.0 + nrm(ks[8], (DEPTH, D_MODEL), 0.02),
        "w_in": nrm(ks[9], (DEPTH, D_MODEL, N_IN), D_MODEL ** -0.5),
        "b_in": b_in,
        "conv_w": nrm(ks[11], (DEPTH, CONV_K, CONV_WIDTH), CONV_K ** -0.5),
        "conv_b": nrm(ks[12], (DEPTH, CONV_WIDTH), 0.01),
        "conv_ln_g": 1.0 + nrm(ks[13], (DEPTH, CONV_WIDTH), 0.02),
        "conv_ln_b": nrm(ks[14], (DEPTH, CONV_WIDTH), 0.01),
        "w_pc": nrm(ks[15], (DEPTH, CONV_WIDTH, D_MODEL), CONV_WIDTH ** -0.5),
        "m_norm_g": 1.0 + nrm(ks[16], (DEPTH, M_WIDTH), 0.02),
        "w_pm": nrm(ks[17], (DEPTH, M_WIDTH, D_MODEL), M_WIDTH ** -0.5),
        "cq_g": 1.0 + nrm(ks[18], (DEPTH, Q_LORA), 0.02),
        "ckv_g": 1.0 + nrm(ks[19], (DEPTH, KV_LORA), 0.02),
        "qn_g": 1.0 + nrm(ks[20], (DEPTH, NOPE_DIM), 0.02),
        "qr_g": 1.0 + nrm(ks[21], (DEPTH, ROPE_DIM), 0.02),
        "kn_g": 1.0 + nrm(ks[22], (DEPTH, NOPE_DIM), 0.02),
        "kr_g": 1.0 + nrm(ks[23], (DEPTH, ROPE_DIM), 0.02),
        "w_uq": nrm(ks[24], (DEPTH, Q_LORA, A_HEADS * (NOPE_DIM + ROPE_DIM)), Q_LORA ** -0.5),
        "w_ukv": nrm(ks[25], (DEPTH, KV_LORA, A_HEADS * (NOPE_DIM + V_DIM)), KV_LORA ** -0.5),
        "w_pa": nrm(ks[26], (DEPTH, A_WIDTH, D_MODEL), A_WIDTH ** -0.5),
        "w_out": nrm(ks[27], (DEPTH, D_MODEL, D_MODEL), D_MODEL ** -0.5),
    }


def reference(x_prompt, x_sample, cache_kv_latent, cache_k_rope, state_conv, state_mlstm_C, state_mlstm_n,
              state_mlstm_m, ln_g, w_in, b_in, conv_w, conv_b, conv_ln_g, conv_ln_b, w_pc, m_norm_g, w_pm,
              cq_g, ckv_g, qn_g, qr_g, kn_g, kr_g, w_uq, w_ukv, w_pa, w_out):
    Bp, Lp, _ = x_prompt.shape
    Bs, Ls, _ = x_sample.shape
    past = cache_kv_latent.shape[2]
    pos_p = jnp.arange(Lp, dtype=jnp.int32)
    pos_s = past + jnp.arange(Ls, dtype=jnp.int32)
    xp, xs = x_prompt, x_sample
    cp_l, cs_l, Cp_l, np_l, mp_l, Cs_l, ns_l, ms_l, latp_l, krp_l, lats_l, krs_l = ([] for _ in range(12))
    for l in range(DEPTH):
        p = (ln_g[l], w_in[l], b_in[l], conv_w[l], conv_b[l], conv_ln_g[l], conv_ln_b[l], w_pc[l], m_norm_g[l],
             w_pm[l], cq_g[l], ckv_g[l], qn_g[l], qr_g[l], kn_g[l], kr_g[l], w_uq[l], w_ukv[l], w_pa[l], w_out[l])
        xp, cst, Cn, nn_, mn_, lat, kr = mixer_layer(
            xp, pos_p, jnp.zeros((Bp, CONV_STATE, CONV_WIDTH), xp.dtype),
            jnp.zeros((Bp, M_HEADS, M_DK, M_DV), jnp.float32), jnp.zeros((Bp, M_HEADS, M_DK), jnp.float32),
            jnp.zeros((Bp, M_HEADS), jnp.float32), None, None, *p)
        cp_l.append(cst); Cp_l.append(Cn); np_l.append(nn_); mp_l.append(mn_); latp_l.append(lat); krp_l.append(kr)
        xs, cst, Cn, nn_, mn_, lat, kr = mixer_layer(
            xs, pos_s, state_conv[l], state_mlstm_C[l].astype(jnp.float32), state_mlstm_n[l].astype(jnp.float32),
            state_mlstm_m[l].astype(jnp.float32), cache_kv_latent[l], cache_k_rope[l], *p)
        cs_l.append(cst); Cs_l.append(Cn); ns_l.append(nn_); ms_l.append(mn_); lats_l.append(lat); krs_l.append(kr)
    return (xp, xs,
            jnp.stack(cp_l), jnp.stack(cs_l),
            jnp.stack(Cp_l), jnp.stack(np_l), jnp.stack(mp_l),
            jnp.stack(Cs_l), jnp.stack(ns_l), jnp.stack(ms_l),
            jnp.stack(latp_l), jnp.stack(krp_l),
            jnp.stack(lats_l), jnp.stack(krs_l))
```

```python
import functools

import jax
import jax.numpy as jnp
from jax import lax
from jax.experimental import pallas as pl
from jax.experimental.pallas import tpu as pltpu

CHUNK = 64
EPS = 1e-6
ROPE_THETA = 10000.0
N_BRANCH = 3

LANES = 128
VMEM_LIMIT_BYTES = 56 * 1024 * 1024
ROPE_SLOT = LANES

BF16 = jnp.bfloat16
F32 = jnp.float32


def _tile(n, pref, align):
    t = min(pref, n)
    t -= t % align
    while t >= align:
        if n % t == 0:
            return t
        t -= align
    return n


def _params(*sem):
    return pltpu.CompilerParams(dimension_semantics=sem, vmem_limit_bytes=VMEM_LIMIT_BYTES)


def _rms(x, g, n=None):
    n = x.shape[-1] if n is None else n
    ms = jnp.sum(x * x, axis=-1, keepdims=True) * (1.0 / n)
    return x * lax.rsqrt(ms + EPS) * g


def _silu(x):
    return x * jax.nn.sigmoid(x)


def _rms_kernel(x_ref, g_ref, o_ref):
    o_ref[...] = _rms(x_ref[...], g_ref[...]).astype(o_ref.dtype)


def _rmsnorm_rows(x, g, tm=512):
    m, d = x.shape
    tm = _tile(m, tm, 16)
    return pl.pallas_call(
        _rms_kernel,
        grid=(m // tm,),
        in_specs=[pl.BlockSpec((tm, d), lambda i: (i, 0)), pl.BlockSpec((1, d), lambda i: (0, 0))],
        out_specs=pl.BlockSpec((tm, d), lambda i: (i, 0)),
        out_shape=jax.ShapeDtypeStruct((m, d), BF16),
        compiler_params=_params("parallel"),
        name="rmsnorm",
    )(x, g.reshape(1, d))


def _mm_kernel(*refs, has_bias, has_res):
    a_ref, w_ref = refs[0], refs[1]
    o_ref = refs[-1]
    acc = jnp.dot(a_ref[...], w_ref[...], preferred_element_type=F32)
    k = 2
    if has_bias:
        acc = acc + refs[k][...]
        k += 1
    if has_res:
        acc = acc + refs[k][...]
    o_ref[...] = acc.astype(o_ref.dtype)


def _matmul(a, w, bias=None, res=None, out_dtype=F32, tm=1024, tn=512, name="matmul"):
    m, k = a.shape
    n = w.shape[1]
    tm = _tile(m, tm, 16)
    tn = _tile(n, tn, LANES)
    in_specs = [pl.BlockSpec((tm, k), lambda i, j: (i, 0)), pl.BlockSpec((k, tn), lambda i, j: (0, j))]
    args = [a, w]
    if bias is not None:
        in_specs.append(pl.BlockSpec((1, tn), lambda i, j: (0, j)))
        args.append(bias.reshape(1, n))
    if res is not None:
        in_specs.append(pl.BlockSpec((tm, tn), lambda i, j: (i, j)))
        args.append(res)
    return pl.pallas_call(
        functools.partial(_mm_kernel, has_bias=bias is not None, has_res=res is not None),
        grid=(m // tm, n // tn),
        in_specs=in_specs,
        out_specs=pl.BlockSpec((tm, tn), lambda i, j: (i, j)),
        out_shape=jax.ShapeDtypeStruct((m, n), out_dtype),
        compiler_params=_params("parallel", "parallel"),
        name=name,
    )(*args)


def _conv_kernel(*refs, tl, ks, hp, rc, has_state):
    if has_state:
        ca_ref, cb_ref, cz_ref, prev_ref, w_ref, b_ref, lg_ref, lb_ref, cv_ref, st_ref, ubuf = refs
    else:
        ca_ref, cb_ref, cz_ref, w_ref, b_ref, lg_ref, lb_ref, cv_ref, st_ref, ubuf = refs
    t = pl.program_id(1)
    c = ubuf.shape[1]

    @pl.when(t == 0)
    def _():
        ubuf[0:hp, :] = jnp.zeros((hp, c), F32)
        if has_state:
            ubuf[hp - ks:hp, :] = prev_ref[0]

    @pl.when(t > 0)
    def _():
        ubuf[0:hp, :] = ubuf[tl:tl + hp, :]

    ubuf[hp:hp + tl, :] = ca_ref[...] * jax.nn.sigmoid(cb_ref[...])

    bias = b_ref[...]
    lg = lg_ref[...]
    lb = lb_ref[...]
    for r in range(0, tl, rc):
        acc = jnp.broadcast_to(bias, (rc, c))
        for j in range(ks + 1):
            s = hp - ks + r + j
            acc = acc + w_ref[j:j + 1, :] * ubuf[s:s + rc, :]
        mu = jnp.mean(acc, axis=-1, keepdims=True)
        d = acc - mu
        var = jnp.mean(d * d, axis=-1, keepdims=True)
        y = d * lax.rsqrt(var + EPS) * lg + lb
        cv_ref[r:r + rc, :] = (_silu(y) * _silu(cz_ref[r:r + rc, :])).astype(cv_ref.dtype)

    @pl.when(t == pl.num_programs(1) - 1)
    def _():
        st_ref[0] = ubuf[hp + tl - ks:hp + tl, :]


def _conv_branch(proj, cols, row0, nb, seq, prev, w, b, lg, lb):
    c_a, c_b, c_z = cols
    kk, c = w.shape
    ks = kk - 1
    hp = -(-ks // 8) * 8
    tl = _tile(seq, 128, 32)
    rc = 32
    assert tl % rc == 0 and tl >= hp and row0 % tl == 0 and seq >= ks
    nt = seq // tl
    rb0 = row0 // tl
    row_spec = lambda off: pl.BlockSpec((tl, c), lambda bi, ti, off=off: (rb0 + bi * nt + ti, off // c))
    vec_spec = pl.BlockSpec((1, c), lambda bi, ti: (0, 0))
    in_specs = [row_spec(c_a), row_spec(c_b), row_spec(c_z)]
    args = [proj, proj, proj]
    if prev is not None:
        in_specs.append(pl.BlockSpec((1, ks, c), lambda bi, ti: (bi, 0, 0)))
        args.append(prev)
    in_specs += [pl.BlockSpec((kk, c), lambda bi, ti: (0, 0)), vec_spec, vec_spec, vec_spec]
    args += [w, b.reshape(1, c), lg.reshape(1, c), lb.reshape(1, c)]
    return pl.pallas_call(
        functools.partial(_conv_kernel, tl=tl, ks=ks, hp=hp, rc=rc, has_state=prev is not None),
        grid=(nb, nt),
        in_specs=in_specs,
        out_specs=[pl.BlockSpec((tl, c), lambda bi, ti: (bi * nt + ti, 0)),
                   pl.BlockSpec((1, ks, c), lambda bi, ti: (bi, 0, 0))],
        out_shape=[jax.ShapeDtypeStruct((nb * seq, c), BF16), jax.ShapeDtypeStruct((nb, ks, c), F32)],
        scratch_shapes=[pltpu.VMEM((hp + tl, c), F32)],
        compiler_params=_params("parallel", "arbitrary"),
        name="conv_branch",
    )(*args)


def _mlstm_kernel(*refs, L, H, dk, dv, has_state):
    if has_state:
        (q_ref, k_ref, v_ref, g_ref, o_ref, z_ref, ng_ref, c0_ref, n0_ref, m0_ref,
         hm_ref, cs_ref, ns_ref, ms_ref) = refs
    else:
        q_ref, k_ref, v_ref, g_ref, o_ref, z_ref, ng_ref, hm_ref, cs_ref, ns_ref, ms_ref = refs

    @pl.when(pl.program_id(1) == 0)
    def _():
        if has_state:
            cs_ref[...] = c0_ref[...]
            ns_ref[...] = n0_ref[...]
            ms_ref[...] = m0_ref[...]
        else:
            cs_ref[...] = jnp.zeros(cs_ref.shape, F32)
            ns_ref[...] = jnp.zeros(ns_ref.shape, F32)
            ms_ref[...] = jnp.zeros(ms_ref.shape, F32)

    row = lax.broadcasted_iota(jnp.int32, (L, L), 0)
    col = lax.broadcasted_iota(jnp.int32, (L, L), 1)
    eye = row == col
    tril = col <= row
    gates = g_ref[...]
    glane = lax.broadcasted_iota(jnp.int32, gates.shape, 1)
    neg_inf = jnp.float32(-jnp.inf)

    for h in range(H):
        li_col = jnp.sum(jnp.where(glane == h, gates, 0.0), axis=1, keepdims=True)
        f_col = jnp.sum(jnp.where(glane == H + h, gates, 0.0), axis=1, keepdims=True)
        lf_col = jnp.minimum(f_col, 0.0) - jnp.log(1.0 + jnp.exp(-jnp.abs(f_col)))
        lf_row = jnp.sum(jnp.where(eye, lf_col, 0.0), axis=0, keepdims=True)
        li_row = jnp.sum(jnp.where(eye, li_col, 0.0), axis=0, keepdims=True)
        bt_col = jnp.sum(jnp.where(tril, lf_row, 0.0), axis=1, keepdims=True)
        bt_row = jnp.sum(jnp.where(row <= col, lf_col, 0.0), axis=0, keepdims=True)

        m_prev = ms_ref[0, 0:1, h:h + 1]
        c_prev = cs_ref[0, h]
        n_prev = ns_ref[0, h:h + 1, :]

        q = q_ref[:, h * dk:(h + 1) * dk]
        k = k_ref[:, h * dk:(h + 1) * dk] * (dk ** -0.5)
        v = v_ref[:, h * dv:(h + 1) * dv]
        qb = q.astype(BF16)
        kb = k.astype(BF16)
        vb = v.astype(BF16)

        logw = jnp.where(tril, bt_col - bt_row + li_row, neg_inf)
        a_col = bt_col + m_prev
        m_t = jnp.maximum(a_col, jnp.max(logw, axis=1, keepdims=True))
        w_inter = jnp.exp(a_col - m_t)
        qk = lax.dot_general(qb, kb, (((1,), (1,)), ((), ())), preferred_element_type=F32)
        sw = jnp.exp(logw - m_t) * qk
        num = (jnp.dot(sw.astype(BF16), vb, preferred_element_type=F32)
               + jnp.dot(qb, c_prev.astype(BF16), preferred_element_type=F32) * w_inter)
        qn = jnp.sum(qb.astype(F32) * n_prev.astype(BF16).astype(F32), axis=1, keepdims=True)
        den = jnp.sum(sw, axis=1, keepdims=True) + w_inter * qn
        denom = jnp.maximum(jnp.abs(den), jnp.exp(-m_t))
        hh = num / denom

        b_last = bt_col[L - 1:L, :]
        gs_col = b_last - bt_col + li_col
        m_new = jnp.maximum(b_last + m_prev, jnp.max(gs_col, axis=0, keepdims=True))
        ws_col = jnp.exp(gs_col - m_new)
        decay = jnp.exp(b_last + m_prev - m_new)
        kw = ws_col * k
        cs_ref[0, h] = decay * c_prev + lax.dot_general(
            kw.astype(BF16), vb, (((0,), (0,)), ((), ())), preferred_element_type=F32)
        ns_ref[0, h:h + 1, :] = decay * n_prev + jnp.sum(kw, axis=0, keepdims=True)
        ms_ref[0, 0:1, h:h + 1] = m_new

        sl = slice(h * dv, (h + 1) * dv)
        hn = _rms(hh, ng_ref[:, sl])
        hm_ref[:, sl] = (hn * jax.nn.sigmoid(o_ref[:, sl]) * _silu(z_ref[:, sl])).astype(hm_ref.dtype)


def _mlstm_branch(proj, cols, row0, nb, seq, state, norm_g, H, dk, dv):
    m_q, m_k, m_v, m_if, m_o, m_z = cols
    L = min(seq, CHUNK)
    assert seq % L == 0 and row0 % L == 0
    nc = seq // L
    rb0 = row0 // L
    spec = lambda off, width: pl.BlockSpec((L, width), lambda bi, ci, off=off, width=width: (rb0 + bi * nc + ci, off // width))
    in_specs = [spec(m_q, H * dk), spec(m_k, H * dk), spec(m_v, H * dv), spec(m_if, LANES), spec(m_o, H * dv),
                spec(m_z, H * dv), pl.BlockSpec((1, H * dv), lambda bi, ci: (0, 0))]
    args = [proj] * 6 + [norm_g.reshape(1, H * dv)]
    st_specs = [pl.BlockSpec((1, H, dk, dv), lambda bi, ci: (bi, 0, 0, 0)),
                pl.BlockSpec((1, H, dk), lambda bi, ci: (bi, 0, 0)),
                pl.BlockSpec((1, 1, H), lambda bi, ci: (bi, 0, 0))]
    if state is not None:
        c0, n0, m0 = state
        in_specs += st_specs
        args += [c0, n0, m0.reshape(nb, 1, H)]
    hm, cs, ns, ms = pl.pallas_call(
        functools.partial(_mlstm_kernel, L=L, H=H, dk=dk, dv=dv, has_state=state is not None),
        grid=(nb, nc),
        in_specs=in_specs,
        out_specs=[pl.BlockSpec((L, H * dv), lambda bi, ci: (bi * nc + ci, 0))] + st_specs,
        out_shape=[jax.ShapeDtypeStruct((nb * seq, H * dv), BF16),
                   jax.ShapeDtypeStruct((nb, H, dk, dv), F32),
                   jax.ShapeDtypeStruct((nb, H, dk), F32),
                   jax.ShapeDtypeStruct((nb, 1, H), F32)],
        compiler_params=_params("parallel", "arbitrary"),
        name="mlstm_branch",
    )(*args)
    return hm, cs, ns, ms.reshape(nb, H)


def _rope_spread(x, cos, sin):
    return x * cos + pltpu.roll(x, ROPE_SLOT // 2, 1) * sin


def _q_kernel(a_ref, cg_ref, w_ref, qn_ref, qr_ref, cos_ref, sin_ref, o_ref, *, hg, nope, rope_dim, scale):
    hq = _rms(a_ref[...], cg_ref[...]).astype(BF16)
    qa = jnp.dot(hq, w_ref[...], preferred_element_type=F32)
    cos = cos_ref[...]
    sin = sin_ref[...]
    slot = nope + ROPE_SLOT
    for h in range(hg):
        b0 = h * slot
        xn = _rms(qa[:, b0:b0 + nope], qn_ref[...])
        xr = _rope_spread(_rms(qa[:, b0 + nope:b0 + slot], qr_ref[...], rope_dim), cos, sin)
        o_ref[:, b0:b0 + nope] = (xn * scale).astype(o_ref.dtype)
        o_ref[:, b0 + nope:b0 + slot] = (xr * scale).astype(o_ref.dtype)


def _q_proj(proj, a_cq, cq_g, w_uq_slots, qn_g, qr_g_spread, cos, sin, heads, nope, rope_dim, scale):
    m = proj.shape[0]
    ql = cq_g.shape[0]
    slot = nope + ROPE_SLOT
    hg = _tile(heads, 4, 1)
    tm = _tile(m, 512, 16)
    vec = lambda n: pl.BlockSpec((1, n), lambda i, j: (0, 0))
    return pl.pallas_call(
        functools.partial(_q_kernel, hg=hg, nope=nope, rope_dim=rope_dim, scale=scale),
        grid=(m // tm, heads // hg),
        in_specs=[pl.BlockSpec((tm, ql), lambda i, j: (i, a_cq // ql)), vec(ql),
                  pl.BlockSpec((ql, hg * slot), lambda i, j: (0, j)), vec(nope), vec(ROPE_SLOT),
                  pl.BlockSpec((tm, ROPE_SLOT), lambda i, j: (i, 0)), pl.BlockSpec((tm, ROPE_SLOT), lambda i, j: (i, 0))],
        out_specs=pl.BlockSpec((tm, hg * slot), lambda i, j: (i, j)),
        out_shape=jax.ShapeDtypeStruct((m, heads * slot), BF16),
        compiler_params=_params("parallel", "parallel"),
        name="q_proj",
    )(proj, cq_g.reshape(1, ql), w_uq_slots, qn_g.reshape(1, nope), qr_g_spread.reshape(1, ROPE_SLOT), cos, sin)


def _latent_kernel(c_ref, r_ref, cg_ref, rg_ref, cos_ref, sin_ref, lat_ref, kr_ref, *, rope_dim):
    lat_ref[...] = _rms(c_ref[...], cg_ref[...])
    kr_ref[...] = _rope_spread(_rms(r_ref[...], rg_ref[...], rope_dim), cos_ref[...], sin_ref[...])


def _latent_norm(proj, a_ckv, a_kr, ckv_g, kr_g_spread, cos, sin, rope_dim):
    m = proj.shape[0]
    kvl = ckv_g.shape[0]
    tm = _tile(m, 1024, 8)
    rows = lambda n: pl.BlockSpec((tm, n), lambda i: (i, 0))
    return pl.pallas_call(
        functools.partial(_latent_kernel, rope_dim=rope_dim),
        grid=(m // tm,),
        in_specs=[pl.BlockSpec((tm, kvl), lambda i: (i, a_ckv // kvl)),
                  pl.BlockSpec((tm, ROPE_SLOT), lambda i: (i, a_kr // ROPE_SLOT)),
                  pl.BlockSpec((1, kvl), lambda i: (0, 0)), pl.BlockSpec((1, ROPE_SLOT), lambda i: (0, 0)),
                  rows(ROPE_SLOT), rows(ROPE_SLOT)],
        out_specs=[rows(kvl), rows(ROPE_SLOT)],
        out_shape=[jax.ShapeDtypeStruct((m, kvl), F32), jax.ShapeDtypeStruct((m, ROPE_SLOT), F32)],
        compiler_params=_params("parallel"),
        name="latent_norm",
    )(proj, proj, ckv_g.reshape(1, kvl), kr_g_spread.reshape(1, ROPE_SLOT), cos, sin)


def _kv_kernel(lat_ref, kr_ref, w_ref, kn_ref, kc_ref, v_ref, *, hg, nope, vd):
    kv = jnp.dot(lat_ref[...].astype(BF16), w_ref[...], preferred_element_type=F32)
    kr = kr_ref[...].astype(kc_ref.dtype)
    slot = nope + ROPE_SLOT
    for h in range(hg):
        b0 = h * (nope + vd)
        kc_ref[:, h * slot:h * slot + nope] = _rms(kv[:, b0:b0 + nope], kn_ref[...]).astype(kc_ref.dtype)
        kc_ref[:, h * slot + nope:(h + 1) * slot] = kr
        v_ref[:, h * vd:(h + 1) * vd] = kv[:, b0 + nope:b0 + nope + vd].astype(v_ref.dtype)


def _kv_up(lat, kr_spread, w_ukv, kn_g, heads, nope, vd, tm):
    rows, kvl = lat.shape
    slot = nope + ROPE_SLOT
    hg = _tile(heads, 4, 1)
    assert rows % tm == 0
    return pl.pallas_call(
        functools.partial(_kv_kernel, hg=hg, nope=nope, vd=vd),
        grid=(rows // tm, heads // hg),
        in_specs=[pl.BlockSpec((tm, kvl), lambda i, j: (i, 0)), pl.BlockSpec((tm, ROPE_SLOT), lambda i, j: (i, 0)),
                  pl.BlockSpec((kvl, hg * (nope + vd)), lambda i, j: (0, j)), pl.BlockSpec((1, nope), lambda i, j: (0, 0))],
        out_specs=[pl.BlockSpec((tm, hg * slot), lambda i, j: (i, j)), pl.BlockSpec((tm, hg * vd), lambda i, j: (i, j))],
        out_shape=[jax.ShapeDtypeStruct((rows, heads * slot), BF16), jax.ShapeDtypeStruct((rows, heads * vd), BF16)],
        compiler_params=_params("parallel", "parallel"),
        name="kv_up",
    )(lat, kr_spread, w_ukv, kn_g.reshape(1, nope))


def _softmax_step(q, k, v, m, l, acc, mask):
    s = lax.dot_general(q, k, (((1,), (1,)), ((), ())), preferred_element_type=F32)
    if mask is not None:
        s = jnp.where(mask, s, -jnp.inf)
    m_new = jnp.maximum(m, jnp.max(s, axis=1, keepdims=True))
    alpha = jnp.exp(m - m_new)
    p = jnp.exp(s - m_new)
    l = alpha * l + jnp.sum(p, axis=1, keepdims=True)
    acc = alpha * acc + jnp.dot(p.astype(v.dtype), v, preferred_element_type=F32)
    return m_new, l, acc


def _attn_prefill_kernel(q_ref, k_ref, v_ref, z_ref, o_ref, *, tq, vd):
    i = pl.program_id(2)
    q = q_ref[...]

    def body(j, carry):
        off = pl.multiple_of(j * tq, tq)
        return _softmax_step(q, k_ref[pl.ds(off, tq), :], v_ref[pl.ds(off, tq), :], *carry, None)

    init = (jnp.full((tq, 1), -jnp.inf, F32), jnp.zeros((tq, 1), F32), jnp.zeros((tq, vd), F32))
    carry = lax.fori_loop(0, i, body, init)
    off = pl.multiple_of(i * tq, tq)
    rchunk = lax.broadcasted_iota(jnp.int32, (tq, tq), 0) // CHUNK
    cchunk = lax.broadcasted_iota(jnp.int32, (tq, tq), 1) // CHUNK
    m, l, acc = _softmax_step(q, k_ref[pl.ds(off, tq), :], v_ref[pl.ds(off, tq), :], *carry, cchunk <= rchunk)
    o_ref[...] = (acc / l * _silu(z_ref[...])).astype(o_ref.dtype)


def _attn_prefill(qc, kc, vv, proj, a_z, nb, seq, heads, nope, vd):
    slot = nope + ROPE_SLOT
    tq = _tile(seq, 256, CHUNK)
    nq = seq // tq
    return pl.pallas_call(
        functools.partial(_attn_prefill_kernel, tq=tq, vd=vd),
        grid=(nb, heads, nq),
        in_specs=[pl.BlockSpec((tq, slot), lambda b, h, i: (b * nq + i, h)),
                  pl.BlockSpec((seq, slot), lambda b, h, i: (b, h)),
                  pl.BlockSpec((seq, vd), lambda b, h, i: (b, h)),
                  pl.BlockSpec((tq, vd), lambda b, h, i: (b * nq + i, a_z // vd + h))],
        out_specs=pl.BlockSpec((tq, vd), lambda b, h, i: (b * nq + i, h)),
        out_shape=jax.ShapeDtypeStruct((nb * seq, heads * vd), BF16),
        compiler_params=_params("parallel", "parallel", "arbitrary"),
        name="attn_prefill",
    )(qc, kc, vv, proj)


def _attn_decode_kernel(q_ref, k_ref, v_ref, z_ref, o_ref, *, past, vd):
    q = q_ref[...]
    k = k_ref[...]
    lq, t = q.shape[0], k.shape[0]
    qchunk = (past + lax.broadcasted_iota(jnp.int32, (lq, t), 0)) // CHUNK
    kchunk = lax.broadcasted_iota(jnp.int32, (lq, t), 1) // CHUNK
    init = (jnp.full((lq, 1), -jnp.inf, F32), jnp.zeros((lq, 1), F32), jnp.zeros((lq, vd), F32))
    m, l, acc = _softmax_step(q, k, v_ref[...], *init, kchunk <= qchunk)
    o_ref[...] = (acc / l * _silu(z_ref[...])).astype(o_ref.dtype)


def _attn_decode(qc, kc, vv, proj, a_z, row0, nb, seq, past, heads, nope, vd):
    slot = nope + ROPE_SLOT
    t = past + seq
    assert row0 % seq == 0
    rb0 = row0 // seq
    return pl.pallas_call(
        functools.partial(_attn_decode_kernel, past=past, vd=vd),
        grid=(nb, heads),
        in_specs=[pl.BlockSpec((seq, slot), lambda b, h: (rb0 + b, h)),
                  pl.BlockSpec((t, slot), lambda b, h: (b, h)),
                  pl.BlockSpec((t, vd), lambda b, h: (b, h)),
                  pl.BlockSpec((seq, vd), lambda b, h: (rb0 + b, a_z // vd + h))],
        out_specs=pl.BlockSpec((seq, vd), lambda b, h: (b, h)),
        out_shape=jax.ShapeDtypeStruct((nb * seq, heads * vd), BF16),
        compiler_params=_params("parallel", "parallel"),
        name="attn_decode",
    )(qc, kc, vv, proj)


def _merge_kernel(cv_ref, hm_ref, az_ref, wc_ref, wm_ref, wa_ref, gc_ref, gm_ref, ga_ref, o_ref):
    yc = jnp.dot(cv_ref[...], wc_ref[...], preferred_element_type=F32)
    ym = jnp.dot(hm_ref[...], wm_ref[...], preferred_element_type=F32)
    ya = jnp.dot(az_ref[...], wa_ref[...], preferred_element_type=F32)
    y = (jax.nn.sigmoid(gc_ref[...].astype(F32)) * yc + jax.nn.sigmoid(gm_ref[...].astype(F32)) * ym
         + jax.nn.sigmoid(ga_ref[...].astype(F32)) * ya)
    o_ref[...] = y.astype(o_ref.dtype)


def _merge(cv, hm, az, w_pc, w_pm, w_pa, gates, d):
    m = cv.shape[0]
    tm = _tile(m, 1024, 16)
    tn = _tile(d, 512, LANES)
    nj = d // tn
    rows = lambda a: pl.BlockSpec((tm, a.shape[1]), lambda i, j: (i, 0))
    wcol = lambda w: pl.BlockSpec((w.shape[0], tn), lambda i, j: (0, j))
    gate = lambda b: pl.BlockSpec((tm, tn), lambda i, j, b=b: (i, b * nj + j))
    return pl.pallas_call(
        _merge_kernel,
        grid=(m // tm, nj),
        in_specs=[rows(cv), rows(hm), rows(az), wcol(w_pc), wcol(w_pm), wcol(w_pa), gate(0), gate(1), gate(2)],
        out_specs=pl.BlockSpec((tm, tn), lambda i, j: (i, j)),
        out_shape=jax.ShapeDtypeStruct((m, d), BF16),
        compiler_params=_params("parallel", "parallel"),
        name="merge",
    )(cv, hm, az, w_pc, w_pm, w_pa, gates, gates, gates)


def _spread_cols(a, half):
    z = jnp.zeros(a.shape[:-1] + (ROPE_SLOT // 2 - half,), a.dtype)
    return jnp.concatenate([a[..., :half], z, a[..., half:], z], axis=-1)


def _unspread_cols(a, half):
    return jnp.concatenate([a[..., :half], a[..., ROPE_SLOT // 2:ROPE_SLOT // 2 + half]], axis=-1)


def kernel(x_prompt, x_sample, cache_kv_latent, cache_k_rope, state_conv, state_mlstm_C, state_mlstm_n, state_mlstm_m, ln_g, w_in, b_in, conv_w, conv_b, conv_ln_g, conv_ln_b, w_pc, m_norm_g, w_pm, cq_g, ckv_g, qn_g, qr_g, kn_g, kr_g, w_uq, w_ukv, w_pa, w_out):
    bp, lp, d = x_prompt.shape
    bs, ls, _ = x_sample.shape
    depth = w_in.shape[0]
    past = cache_kv_latent.shape[2]
    kvl = cache_kv_latent.shape[3]
    rope_dim = cache_k_rope.shape[3]
    half = rope_dim // 2
    cw = conv_w.shape[2]
    H, dk, dv = state_mlstm_C.shape[2:]
    mw = H * dv
    ql = cq_g.shape[1]
    nope = qn_g.shape[1]
    heads = w_uq.shape[2] // (nope + rope_dim)
    vd = w_ukv.shape[2] // heads - nope
    aw = heads * vd
    scale = (nope + rope_dim) ** -0.5
    assert rope_dim % 2 == 0 and half <= ROPE_SLOT // 2 and 2 * H <= LANES
    mp, msz = bp * lp, bs * ls
    m = mp + msz

    sizes = (cw, cw, cw, H * dk, H * dk, mw, H, H, mw, mw, ql, kvl, rope_dim, aw, N_BRANCH * d)
    offs = [0]
    for s in sizes:
        offs.append(offs[-1] + s)
    (o_ca, o_cb, o_cz, o_mq, o_mk, o_mv, o_mi, o_mf, o_mo, o_mz, o_cq, o_ckv, o_kr, o_az, o_g) = offs[:-1]
    assert offs[-1] == w_in.shape[2]
    front = [("ca", o_ca, cw), ("cb", o_cb, cw), ("cz", o_cz, cw), ("mv", o_mv, mw), ("mo", o_mo, mw), ("mz", o_mz, mw),
             ("cq", o_cq, ql), ("az", o_az, aw), ("mq", o_mq, H * dk), ("mk", o_mk, H * dk), ("ckv", o_ckv, kvl)]
    col = {}
    pos_ = 0
    for name, _, width in front:
        col[name] = pos_
        pos_ += width
    col["if"] = pos_
    col["kr"] = pos_ + LANES
    nf_used = pos_ + 2 * LANES
    nf = -(-nf_used // 1024) * 1024 if nf_used > 1024 else nf_used
    for name, width in (("ca", cw), ("cb", cw), ("cz", cw), ("mv", mw), ("mo", mw), ("mz", mw), ("cq", ql), ("az", vd),
                        ("mq", H * dk), ("mk", H * dk), ("ckv", kvl)):
        assert col[name] % width == 0, (name, col[name], width)

    def pack_front(w):
        lead = w.shape[:-1]
        parts = [w[..., o:o + width] for _, o, width in front]
        parts.append(w[..., o_mi:o_mi + 2 * H])
        parts.append(jnp.zeros(lead + (LANES - 2 * H,), w.dtype))
        parts.append(_spread_cols(w[..., o_kr:o_kr + rope_dim], half))
        parts.append(jnp.zeros(lead + (nf - nf_used,), w.dtype))
        return jnp.concatenate(parts, axis=-1)

    pos = jnp.concatenate([jnp.tile(jnp.arange(lp, dtype=jnp.int32), bp),
                           jnp.tile(past + jnp.arange(ls, dtype=jnp.int32), bs)])
    freqs = ROPE_THETA ** (-jnp.arange(half, dtype=F32) / half)
    ang = pos.astype(F32)[:, None] * freqs
    cos_t = _spread_cols(jnp.concatenate([jnp.cos(ang), jnp.cos(ang)], axis=-1), half)
    sin_t = _spread_cols(jnp.concatenate([-jnp.sin(ang), jnp.sin(ang)], axis=-1), half)

    x = jnp.concatenate([x_prompt.reshape(mp, d), x_sample.reshape(msz, d)], axis=0)
    outs = [[] for _ in range(12)]
    for l in range(depth):
        w_front = pack_front(w_in[l]).astype(BF16)
        b_front = pack_front(b_in[l])
        w_gate = w_in[l][:, o_g:].astype(BF16)
        b_gate = b_in[l][o_g:]
        w_uq_slots = jnp.concatenate(
            [w_uq[l].reshape(ql, heads, nope + rope_dim)[..., :nope],
             _spread_cols(w_uq[l].reshape(ql, heads, nope + rope_dim)[..., nope:], half)], axis=-1
        ).reshape(ql, heads * (nope + ROPE_SLOT)).astype(BF16)
        qr_g_spread = _spread_cols(qr_g[l], half)
        kr_g_spread = _spread_cols(kr_g[l], half)

        h = _rmsnorm_rows(x, ln_g[l])
        proj = _matmul(h, w_front, bias=b_front, name="in_proj_front")
        gates = _matmul(h, w_gate, bias=b_gate, name="in_proj_gates")

        cols_a = (col["ca"], col["cb"], col["cz"])
        cv_p, cst_p = _conv_branch(proj, cols_a, 0, bp, lp, None, conv_w[l], conv_b[l], conv_ln_g[l], conv_ln_b[l])
        cv_s, cst_s = _conv_branch(proj, cols_a, mp, bs, ls, state_conv[l], conv_w[l], conv_b[l], conv_ln_g[l], conv_ln_b[l])

        cols_b = (col["mq"], col["mk"], col["mv"], col["if"], col["mo"], col["mz"])
        hm_p, c_p, n_p, m_p = _mlstm_branch(proj, cols_b, 0, bp, lp, None, m_norm_g[l], H, dk, dv)
        hm_s, c_s, n_s, m_s = _mlstm_branch(proj, cols_b, mp, bs, ls,
                                            (state_mlstm_C[l], state_mlstm_n[l], state_mlstm_m[l]), m_norm_g[l], H, dk, dv)

        qc = _q_proj(proj, col["cq"], cq_g[l], w_uq_slots, qn_g[l], qr_g_spread, cos_t, sin_t, heads, nope, rope_dim, scale)
        lat, kr_spread = _latent_norm(proj, col["ckv"], col["kr"], ckv_g[l], kr_g_spread, cos_t, sin_t, rope_dim)
        w_ukv_b = w_ukv[l].astype(BF16)
        kc_p, v_p = _kv_up(lat[:mp], kr_spread[:mp], w_ukv_b, kn_g[l], heads, nope, vd, _tile(mp, 1024, 16))
        lat_s = jnp.concatenate([cache_kv_latent[l], lat[mp:].reshape(bs, ls, kvl)], axis=1).reshape(bs * (past + ls), kvl)
        kr_s = jnp.concatenate([_spread_cols(cache_k_rope[l], half), kr_spread[mp:].reshape(bs, ls, ROPE_SLOT)],
                               axis=1).reshape(bs * (past + ls), ROPE_SLOT)
        kc_s, v_s = _kv_up(lat_s, kr_s, w_ukv_b, kn_g[l], heads, nope, vd, past + ls)
        az_p = _attn_prefill(qc, kc_p, v_p, proj, col["az"], bp, lp, heads, nope, vd)
        az_s = _attn_decode(qc, kc_s, v_s, proj, col["az"], mp, bs, ls, past, heads, nope, vd)

        merged = _merge(jnp.concatenate([cv_p, cv_s]), jnp.concatenate([hm_p, hm_s]), jnp.concatenate([az_p, az_s]),
                        w_pc[l].astype(BF16), w_pm[l].astype(BF16), w_pa[l].astype(BF16), gates, d)
        x = _matmul(merged, w_out[l].astype(BF16), res=x, name="out_proj")

        kr_new = _unspread_cols(kr_spread, half)
        for lst, val in zip(outs, (cst_p, cst_s, c_p, n_p, m_p, c_s, n_s, m_s,
                                   lat[:mp].reshape(bp, lp, kvl), kr_new[:mp].reshape(bp, lp, rope_dim),
                                   lat[mp:].reshape(bs, ls, kvl), kr_new[mp:].reshape(bs, ls, rope_dim))):
            lst.append(val)
    return (x[:mp].reshape(bp, lp, d), x[mp:].reshape(bs, ls, d)) + tuple(jnp.stack(o) for o in outs)
```

```python
import functools
import math

import jax
import jax.numpy as jnp
from jax import lax
from jax.experimental import pallas as pl
from jax.experimental.pallas import tpu as pltpu

CHUNK = 64
EPS = 1e-6
ROPE_THETA = 10000.0
N_BRANCH = 3

LANES = 128
SUBLANES = 8
VMEM_LIMIT_BYTES = 56 * 1024 * 1024
ROPE_SLOT = LANES

BF16 = jnp.bfloat16
F32 = jnp.float32


def _tile(n, pref, align):
    t = min(pref, n)
    t -= t % align
    while t >= align:
        if n % t == 0:
            return t
        t -= align
    return n


def _params(*sem):
    return pltpu.CompilerParams(dimension_semantics=sem, vmem_limit_bytes=VMEM_LIMIT_BYTES)


def _rms(x, g, n=None):
    n = x.shape[-1] if n is None else n
    ms = jnp.sum(x * x, axis=-1, keepdims=True) * (1.0 / n)
    return x * lax.rsqrt(ms + EPS) * g


def _silu(x):
    return x * jax.nn.sigmoid(x)


def _col_block(rows, width, off, row_fn):
    assert off % width == 0, (off, width)
    return pl.BlockSpec((rows, width), lambda *g: (row_fn(*g), off // width))


def _rms_kernel(x_ref, g_ref, o_ref):
    o_ref[...] = _rms(x_ref[...], g_ref[...]).astype(o_ref.dtype)


def _rmsnorm_rows(x, g, tm=512):
    m, d = x.shape
    tm = _tile(m, tm, 16)
    return pl.pallas_call(
        _rms_kernel,
        grid=(m // tm,),
        in_specs=[pl.BlockSpec((tm, d), lambda i: (i, 0)), pl.BlockSpec((1, d), lambda i: (0, 0))],
        out_specs=pl.BlockSpec((tm, d), lambda i: (i, 0)),
        out_shape=jax.ShapeDtypeStruct((m, d), BF16),
        compiler_params=_params("parallel"),
        name="rmsnorm",
    )(x, g.reshape(1, d))


def _mm_kernel(*refs, has_bias, has_res):
    a_ref, w_ref = refs[0], refs[1]
    o_ref = refs[-1]
    acc = jnp.dot(a_ref[...], w_ref[...], preferred_element_type=F32)
    k = 2
    if has_bias:
        acc = acc + refs[k][...]
        k += 1
    if has_res:
        acc = acc + refs[k][...]
    o_ref[...] = acc.astype(o_ref.dtype)


def _matmul(a, w, bias=None, res=None, out_dtype=F32, tm=1024, tn=1024, name="matmul"):
    m, k = a.shape
    n = w.shape[1]
    tm = _tile(m, tm, 16)
    tn = _tile(n, tn, LANES)
    in_specs = [pl.BlockSpec((tm, k), lambda i, j: (i, 0)), pl.BlockSpec((k, tn), lambda i, j: (0, j))]
    args = [a, w]
    if bias is not None:
        in_specs.append(pl.BlockSpec((1, tn), lambda i, j: (0, j)))
        args.append(bias.reshape(1, n))
    if res is not None:
        in_specs.append(pl.BlockSpec((tm, tn), lambda i, j: (i, j)))
        args.append(res)
    return pl.pallas_call(
        functools.partial(_mm_kernel, has_bias=bias is not None, has_res=res is not None),
        grid=(m // tm, n // tn),
        in_specs=in_specs,
        out_specs=pl.BlockSpec((tm, tn), lambda i, j: (i, j)),
        out_shape=jax.ShapeDtypeStruct((m, n), out_dtype),
        compiler_params=_params("parallel", "parallel"),
        name=name,
    )(*args)


def _conv_kernel(*refs, tl, ks, hp, rc, lc, nr, has_state):
    if has_state:
        ca_ref, cb_ref, cz_ref, prev_ref, w_ref, b_ref, lg_ref, lb_ref, cv_ref, st_ref, ubuf, ybuf = refs
    else:
        ca_ref, cb_ref, cz_ref, w_ref, b_ref, lg_ref, lb_ref, cv_ref, st_ref, ubuf, ybuf = refs
    t = pl.program_id(1)
    c = ubuf.shape[1]

    @pl.when(t == 0)
    def _():
        ubuf[0:hp, :] = jnp.zeros((hp, c), F32)
        if has_state:
            ubuf[hp - ks:hp, :] = prev_ref[0]

    @pl.when(t > 0)
    def _():
        ubuf[0:hp, :] = ubuf[tl:tl + hp, :]

    ubuf[hp:hp + tl, :] = ca_ref[...] * jax.nn.sigmoid(cb_ref[...])

    for c0 in range(0, c, lc):
        cols = slice(c0, c0 + lc)
        bias = b_ref[:, cols]
        for r in range(0, tl, rc):
            acc = jnp.broadcast_to(bias, (rc, lc))
            for s in range(SUBLANES):
                taps = [j for j in range(ks + 1) if (hp - ks + j) % SUBLANES == s]
                if not taps:
                    continue
                ext = rc + (SUBLANES if s else 0)
                part = None
                for j in taps:
                    a0 = r + (hp - ks + j) - s
                    term = (ubuf[a0:a0 + ext, cols].reshape(ext // SUBLANES, SUBLANES, lc) * w_ref[j, :, cols]
                            ).reshape(ext, lc)
                    part = term if part is None else part + term
                acc = acc + (part[s:s + rc, :] if s else part)
            ybuf[r:r + rc, cols] = acc

    lg = lg_ref[...]
    lb = lb_ref[...]
    for r in range(0, tl, nr):
        rows = slice(r, r + nr)
        conv = ybuf[rows, :]
        mu = jnp.mean(conv, axis=-1, keepdims=True)
        d = conv - mu
        var = jnp.mean(d * d, axis=-1, keepdims=True)
        y = d * lax.rsqrt(var + EPS) * lg + lb
        cv_ref[rows, :] = (_silu(y) * _silu(cz_ref[rows, :])).astype(cv_ref.dtype)

    @pl.when(t == pl.num_programs(1) - 1)
    def _():
        st_ref[0] = ubuf[hp + tl - ks:hp + tl, :]


def _conv_branch(ca, cb, cz, nb, seq, prev, w, b, lg, lb):
    kk, c = w.shape
    ks = kk - 1
    hp = -(-ks // SUBLANES) * SUBLANES
    tl = _tile(seq, 128, 32)
    rc, lc, nr = tl, LANES, 16
    assert tl % rc == 0 and tl % nr == 0 and tl >= hp and seq >= ks
    nt = seq // tl
    row = lambda bi, ti: bi * nt + ti
    vec_spec = pl.BlockSpec((1, c), lambda bi, ti: (0, 0))
    in_specs = [_col_block(tl, c, ca[1], row), _col_block(tl, c, cb[1], row), _col_block(tl, c, cz[1], row)]
    args = [ca[0], cb[0], cz[0]]
    if prev is not None:
        in_specs.append(pl.BlockSpec((1, ks, c), lambda bi, ti: (bi, 0, 0)))
        args.append(prev)
    in_specs += [pl.BlockSpec((kk, SUBLANES, c), lambda bi, ti: (0, 0, 0)), vec_spec, vec_spec, vec_spec]
    args += [jnp.broadcast_to(w[:, None, :], (kk, SUBLANES, c)), b.reshape(1, c), lg.reshape(1, c), lb.reshape(1, c)]
    return pl.pallas_call(
        functools.partial(_conv_kernel, tl=tl, ks=ks, hp=hp, rc=rc, lc=lc, nr=nr, has_state=prev is not None),
        grid=(nb, nt),
        in_specs=in_specs,
        out_specs=[pl.BlockSpec((tl, c), lambda bi, ti: (bi * nt + ti, 0)),
                   pl.BlockSpec((1, ks, c), lambda bi, ti: (bi, 0, 0))],
        out_shape=[jax.ShapeDtypeStruct((nb * seq, c), BF16), jax.ShapeDtypeStruct((nb, ks, c), F32)],
        scratch_shapes=[pltpu.VMEM((hp + tl, c), F32), pltpu.VMEM((tl, c), F32)],
        compiler_params=_params("parallel", "arbitrary"),
        name="conv_branch",
    )(*args)


def _mlstm_kernel(*refs, L, H, dk, dv, has_state):
    if has_state:
        (q_ref, k_ref, v_ref, g_ref, o_ref, z_ref, ng_ref, c0_ref, n0_ref, m0_ref,
         hm_ref, cs_ref, ns_ref, ms_ref) = refs
    else:
        q_ref, k_ref, v_ref, g_ref, o_ref, z_ref, ng_ref, hm_ref, cs_ref, ns_ref, ms_ref = refs

    @pl.when(pl.program_id(1) == 0)
    def _():
        if has_state:
            cs_ref[...] = c0_ref[...]
            ns_ref[...] = n0_ref[...]
            ms_ref[...] = m0_ref[...]
        else:
            cs_ref[...] = jnp.zeros(cs_ref.shape, F32)
            ns_ref[...] = jnp.zeros(ns_ref.shape, F32)
            ms_ref[...] = jnp.zeros(ms_ref.shape, F32)

    row = lax.broadcasted_iota(jnp.int32, (L, L), 0)
    col = lax.broadcasted_iota(jnp.int32, (L, L), 1)
    eye = row == col
    tril = col <= row
    gates = g_ref[...]
    glane = lax.broadcasted_iota(jnp.int32, gates.shape, 1)
    neg_inf = jnp.float32(-jnp.inf)

    for h in range(H):
        li_col = jnp.sum(jnp.where(glane == h, gates, 0.0), axis=1, keepdims=True)
        f_col = jnp.sum(jnp.where(glane == H + h, gates, 0.0), axis=1, keepdims=True)
        lf_col = jnp.minimum(f_col, 0.0) - jnp.log(1.0 + jnp.exp(-jnp.abs(f_col)))
        lf_row = jnp.sum(jnp.where(eye, lf_col, 0.0), axis=0, keepdims=True)
        li_row = jnp.sum(jnp.where(eye, li_col, 0.0), axis=0, keepdims=True)
        bt_col = jnp.sum(jnp.where(tril, lf_row, 0.0), axis=1, keepdims=True)
        bt_row = jnp.sum(jnp.where(row <= col, lf_col, 0.0), axis=0, keepdims=True)

        m_prev = ms_ref[0, 0:1, h:h + 1]
        c_prev = cs_ref[0, h]
        n_prev = ns_ref[0, h:h + 1, :]

        q = q_ref[:, h * dk:(h + 1) * dk]
        k = k_ref[:, h * dk:(h + 1) * dk] * (dk ** -0.5)
        v = v_ref[:, h * dv:(h + 1) * dv]
        qb = q.astype(BF16)
        kb = k.astype(BF16)
        vb = v.astype(BF16)

        logw = jnp.where(tril, bt_col - bt_row + li_row, neg_inf)
        a_col = bt_col + m_prev
        m_t = jnp.maximum(a_col, jnp.max(logw, axis=1, keepdims=True))
        w_inter = jnp.exp(a_col - m_t)
        qk = lax.dot_general(qb, kb, (((1,), (1,)), ((), ())), preferred_element_type=F32)
        sw = jnp.exp(logw - m_t) * qk
        num = (jnp.dot(sw.astype(BF16), vb, preferred_element_type=F32)
               + jnp.dot(qb, c_prev.astype(BF16), preferred_element_type=F32) * w_inter)
        den = jnp.sum(sw, axis=1, keepdims=True) + w_inter * jnp.sum(q * n_prev, axis=1, keepdims=True)
        denom = jnp.maximum(jnp.abs(den), jnp.exp(-m_t))
        hh = num / denom

        b_last = bt_col[L - 1:L, :]
        gs_col = b_last - bt_col + li_col
        m_new = jnp.maximum(b_last + m_prev, jnp.max(gs_col, axis=0, keepdims=True))
        ws_col = jnp.exp(gs_col - m_new)
        decay = jnp.exp(b_last + m_prev - m_new)
        kw = ws_col * k
        cs_ref[0, h] = decay * c_prev + lax.dot_general(
            kw.astype(BF16), vb, (((0,), (0,)), ((), ())), preferred_element_type=F32)
        ns_ref[0, h:h + 1, :] = decay * n_prev + jnp.sum(kw, axis=0, keepdims=True)
        ms_ref[0, 0:1, h:h + 1] = m_new

        sl = slice(h * dv, (h + 1) * dv)
        hn = _rms(hh, ng_ref[:, sl])
        hm_ref[:, sl] = (hn * jax.nn.sigmoid(o_ref[:, sl]) * _silu(z_ref[:, sl])).astype(hm_ref.dtype)


def _mlstm_branch(mq, mk, mv, mif, mo, mz, nb, seq, state, norm_g, H, dk, dv):
    L = min(seq, CHUNK)
    assert seq % L == 0
    nc = seq // L
    row = lambda bi, ci: bi * nc + ci
    ops = ((mq, H * dk), (mk, H * dk), (mv, H * dv), (mif, LANES), (mo, H * dv), (mz, H * dv))
    in_specs = [_col_block(L, width, op[1], row) for op, width in ops]
    in_specs.append(pl.BlockSpec((1, H * dv), lambda bi, ci: (0, 0)))
    args = [op[0] for op, _ in ops] + [norm_g.reshape(1, H * dv)]
    st_specs = [pl.BlockSpec((1, H, dk, dv), lambda bi, ci: (bi, 0, 0, 0)),
                pl.BlockSpec((1, H, dk), lambda bi, ci: (bi, 0, 0)),
                pl.BlockSpec((1, 1, H), lambda bi, ci: (bi, 0, 0))]
    if state is not None:
        c0, n0, m0 = state
        in_specs += st_specs
        args += [c0, n0, m0.reshape(nb, 1, H)]
    hm, cs, ns, ms = pl.pallas_call(
        functools.partial(_mlstm_kernel, L=L, H=H, dk=dk, dv=dv, has_state=state is not None),
        grid=(nb, nc),
        in_specs=in_specs,
        out_specs=[pl.BlockSpec((L, H * dv), lambda bi, ci: (bi * nc + ci, 0))] + st_specs,
        out_shape=[jax.ShapeDtypeStruct((nb * seq, H * dv), BF16),
                   jax.ShapeDtypeStruct((nb, H, dk, dv), F32),
                   jax.ShapeDtypeStruct((nb, H, dk), F32),
                   jax.ShapeDtypeStruct((nb, 1, H), F32)],
        compiler_params=_params("parallel", "arbitrary"),
        name="mlstm_branch",
    )(*args)
    return hm, cs, ns, ms.reshape(nb, H)


def _rope_slot(x, cos, sin, half):
    swapped = pltpu.roll(x, ROPE_SLOT - half, 1) + pltpu.roll(x, half, 1)
    return x * cos + swapped * sin


def _q_kernel(a_ref, cg_ref, w_ref, qn_ref, qr_ref, cos_ref, sin_ref, o_ref, *, hg, nope, rope_dim, scale):
    hq = _rms(a_ref[...], cg_ref[...]).astype(BF16)
    qa = jnp.dot(hq, w_ref[...], preferred_element_type=F32)
    cos = cos_ref[...]
    sin = sin_ref[...]
    slot = nope + ROPE_SLOT
    for h in range(hg):
        b0 = h * slot
        xn = _rms(qa[:, b0:b0 + nope], qn_ref[...])
        xr = _rope_slot(_rms(qa[:, b0 + nope:b0 + slot], qr_ref[...], rope_dim), cos, sin, rope_dim // 2)
        o_ref[:, b0:b0 + nope] = (xn * scale).astype(o_ref.dtype)
        o_ref[:, b0 + nope:b0 + slot] = (xr * scale).astype(o_ref.dtype)


def _q_proj(acq, cq_g, w_uq_slots, qn_g, qr_g_slot, cos, sin, heads, nope, rope_dim, scale):
    m = acq[0].shape[0]
    ql = cq_g.shape[0]
    slot = nope + ROPE_SLOT
    hg = _tile(heads, 4, 1)
    tm = _tile(m, 512, 16)
    vec = lambda n: pl.BlockSpec((1, n), lambda i, j: (0, 0))
    return pl.pallas_call(
        functools.partial(_q_kernel, hg=hg, nope=nope, rope_dim=rope_dim, scale=scale),
        grid=(m // tm, heads // hg),
        in_specs=[_col_block(tm, ql, acq[1], lambda i, j: i), vec(ql),
                  pl.BlockSpec((ql, hg * slot), lambda i, j: (0, j)), vec(nope), vec(ROPE_SLOT),
                  pl.BlockSpec((tm, ROPE_SLOT), lambda i, j: (i, 0)), pl.BlockSpec((tm, ROPE_SLOT), lambda i, j: (i, 0))],
        out_specs=pl.BlockSpec((tm, hg * slot), lambda i, j: (i, j)),
        out_shape=jax.ShapeDtypeStruct((m, heads * slot), BF16),
        compiler_params=_params("parallel", "parallel"),
        name="q_proj",
    )(acq[0], cq_g.reshape(1, ql), w_uq_slots, qn_g.reshape(1, nope), qr_g_slot.reshape(1, ROPE_SLOT), cos, sin)


def _latent_kernel(c_ref, r_ref, cg_ref, rg_ref, cos_ref, sin_ref, lat_ref, kr_ref, *, rope_dim):
    lat_ref[...] = _rms(c_ref[...], cg_ref[...])
    kr_ref[...] = _rope_slot(_rms(r_ref[...], rg_ref[...], rope_dim), cos_ref[...], sin_ref[...], rope_dim // 2)


def _latent_norm(ackv, akr, ckv_g, kr_g_slot, cos, sin, rope_dim):
    m = ackv[0].shape[0]
    kvl = ckv_g.shape[0]
    tm = _tile(m, 1024, 8)
    rows = lambda n: pl.BlockSpec((tm, n), lambda i: (i, 0))
    return pl.pallas_call(
        functools.partial(_latent_kernel, rope_dim=rope_dim),
        grid=(m // tm,),
        in_specs=[_col_block(tm, kvl, ackv[1], lambda i: i), _col_block(tm, ROPE_SLOT, akr[1], lambda i: i),
                  pl.BlockSpec((1, kvl), lambda i: (0, 0)), pl.BlockSpec((1, ROPE_SLOT), lambda i: (0, 0)),
                  rows(ROPE_SLOT), rows(ROPE_SLOT)],
        out_specs=[rows(kvl), rows(ROPE_SLOT)],
        out_shape=[jax.ShapeDtypeStruct((m, kvl), F32), jax.ShapeDtypeStruct((m, ROPE_SLOT), F32)],
        compiler_params=_params("parallel"),
        name="latent_norm",
    )(ackv[0], akr[0], ckv_g.reshape(1, kvl), kr_g_slot.reshape(1, ROPE_SLOT), cos, sin)


def _kv_kernel(lat_ref, kr_ref, w_ref, kn_ref, kc_ref, v_ref, *, hg, nope, vd):
    kv = jnp.dot(lat_ref[...].astype(BF16), w_ref[...], preferred_element_type=F32)
    kr = kr_ref[...].astype(kc_ref.dtype)
    slot = nope + ROPE_SLOT
    for h in range(hg):
        b0 = h * (nope + vd)
        kc_ref[:, h * slot:h * slot + nope] = _rms(kv[:, b0:b0 + nope], kn_ref[...]).astype(kc_ref.dtype)
        kc_ref[:, h * slot + nope:(h + 1) * slot] = kr
        v_ref[:, h * vd:(h + 1) * vd] = kv[:, b0 + nope:b0 + nope + vd].astype(v_ref.dtype)


def _kv_up(lat, kr_slot, w_ukv, kn_g, heads, nope, vd):
    rows, kvl = lat.shape
    slot = nope + ROPE_SLOT
    hg = _tile(heads, 4, 1)
    tm = _tile(rows, 1024, 16)
    return pl.pallas_call(
        functools.partial(_kv_kernel, hg=hg, nope=nope, vd=vd),
        grid=(rows // tm, heads // hg),
        in_specs=[pl.BlockSpec((tm, kvl), lambda i, j: (i, 0)), pl.BlockSpec((tm, ROPE_SLOT), lambda i, j: (i, 0)),
                  pl.BlockSpec((kvl, hg * (nope + vd)), lambda i, j: (0, j)), pl.BlockSpec((1, nope), lambda i, j: (0, 0))],
        out_specs=[pl.BlockSpec((tm, hg * slot), lambda i, j: (i, j)), pl.BlockSpec((tm, hg * vd), lambda i, j: (i, j))],
        out_shape=[jax.ShapeDtypeStruct((rows, heads * slot), BF16), jax.ShapeDtypeStruct((rows, heads * vd), BF16)],
        compiler_params=_params("parallel", "parallel"),
        name="kv_up",
    )(lat, kr_slot, w_ukv, kn_g.reshape(1, nope))


def _scores(q, k):
    return lax.dot_general(q, k, (((1,), (1,)), ((), ())), preferred_element_type=F32)


def _softmax_step(s, v, m, l, acc, mask):
    if mask is not None:
        s = jnp.where(mask, s, -jnp.inf)
    m_new = jnp.maximum(m, jnp.max(s, axis=1, keepdims=True))
    alpha = jnp.exp2(m - m_new)
    p = jnp.exp2(s - m_new)
    l = alpha * l + jnp.sum(p, axis=1, keepdims=True)
    acc = alpha * acc + jnp.dot(p.astype(v.dtype), v, preferred_element_type=F32)
    return m_new, l, acc


def _softmax_init(rows, vd):
    return (jnp.full((rows, 1), -jnp.inf, F32), jnp.zeros((rows, 1), F32), jnp.zeros((rows, vd), F32))


def _attn_prefill_kernel(q_ref, k_ref, v_ref, z_ref, o_ref, *, tq, vd):
    i = pl.program_id(2)
    q = q_ref[...]
    ktile = lambda j: k_ref[pl.ds(pl.multiple_of(j * tq, tq), tq), :]
    vtile = lambda j: v_ref[pl.ds(pl.multiple_of(j * tq, tq), tq), :]

    def body(j, carry):
        state, s_cur = carry
        s_next = _scores(q, ktile(j + 1))
        return _softmax_step(s_cur, vtile(j), *state, None), s_next

    state, s_diag = lax.fori_loop(0, i, body, (_softmax_init(tq, vd), _scores(q, ktile(0))))
    rchunk = lax.broadcasted_iota(jnp.int32, (tq, tq), 0) // CHUNK
    cchunk = lax.broadcasted_iota(jnp.int32, (tq, tq), 1) // CHUNK
    _, l, acc = _softmax_step(s_diag, vtile(i), *state, cchunk <= rchunk)
    o_ref[...] = (acc / l * _silu(z_ref[...])).astype(o_ref.dtype)


def _attn_prefill(qc, kc, vv, az, nb, seq, heads, nope, vd):
    slot = nope + ROPE_SLOT
    tq = _tile(seq, 512, CHUNK)
    nq = seq // tq
    return pl.pallas_call(
        functools.partial(_attn_prefill_kernel, tq=tq, vd=vd),
        grid=(nb, heads, nq),
        in_specs=[pl.BlockSpec((tq, slot), lambda b, h, i: (b * nq + i, h)),
                  pl.BlockSpec((seq, slot), lambda b, h, i: (b, h)),
                  pl.BlockSpec((seq, vd), lambda b, h, i: (b, h)),
                  pl.BlockSpec((tq, vd), lambda b, h, i: (b * nq + i, az[1] // vd + h))],
        out_specs=pl.BlockSpec((tq, vd), lambda b, h, i: (b * nq + i, h)),
        out_shape=jax.ShapeDtypeStruct((nb * seq, heads * vd), BF16),
        compiler_params=_params("parallel", "parallel", "arbitrary"),
        name="attn_prefill",
    )(qc, kc, vv, az[0])


def _attn_decode_kernel(q_ref, kp_ref, vp_ref, kn_ref, vn_ref, z_ref, o_ref, *, past, vd, slot, hg):
    lq = q_ref.shape[0]

    def mask(t, base):
        qchunk = (past + lax.broadcasted_iota(jnp.int32, (lq, t), 0)) // CHUNK
        kchunk = (base + lax.broadcasted_iota(jnp.int32, (lq, t), 1)) // CHUNK
        return kchunk <= qchunk

    mask_past = mask(past, 0)
    mask_new = mask(lq, past)
    for g in range(hg):
        q = q_ref[:, g * slot:(g + 1) * slot]
        ks = slice(g * slot, (g + 1) * slot)
        vs = slice(g * vd, (g + 1) * vd)
        carry = _softmax_step(_scores(q, kp_ref[:, ks]), vp_ref[:, vs], *_softmax_init(lq, vd), mask_past)
        _, l, acc = _softmax_step(_scores(q, kn_ref[:, ks]), vn_ref[:, vs], *carry, mask_new)
        o_ref[:, vs] = (acc / l * _silu(z_ref[:, vs])).astype(o_ref.dtype)


def _attn_decode(qc, kc_past, v_past, kc_new, v_new, az, nb, seq, past, heads, nope, vd):
    slot = nope + ROPE_SLOT
    hg = _tile(heads, 4, 1)
    assert past > 0
    return pl.pallas_call(
        functools.partial(_attn_decode_kernel, past=past, vd=vd, slot=slot, hg=hg),
        grid=(nb, heads // hg),
        in_specs=[pl.BlockSpec((seq, hg * slot), lambda b, h: (b, h)),
                  pl.BlockSpec((past, hg * slot), lambda b, h: (b, h)),
                  pl.BlockSpec((past, hg * vd), lambda b, h: (b, h)),
                  pl.BlockSpec((seq, hg * slot), lambda b, h: (b, h)),
                  pl.BlockSpec((seq, hg * vd), lambda b, h: (b, h)),
                  pl.BlockSpec((seq, hg * vd), lambda b, h: (b, az[1] // (hg * vd) + h))],
        out_specs=pl.BlockSpec((seq, hg * vd), lambda b, h: (b, h)),
        out_shape=jax.ShapeDtypeStruct((nb * seq, heads * vd), BF16),
        compiler_params=_params("parallel", "parallel"),
        name="attn_decode",
    )(qc, kc_past, v_past, kc_new, v_new, az[0])


def _merge_kernel(cv_ref, hm_ref, az_ref, wc_ref, wm_ref, wa_ref, gc_ref, gm_ref, ga_ref, o_ref):
    yc = jnp.dot(cv_ref[...], wc_ref[...], preferred_element_type=F32)
    ym = jnp.dot(hm_ref[...], wm_ref[...], preferred_element_type=F32)
    ya = jnp.dot(az_ref[...], wa_ref[...], preferred_element_type=F32)
    y = (jax.nn.sigmoid(gc_ref[...].astype(F32)) * yc + jax.nn.sigmoid(gm_ref[...].astype(F32)) * ym
         + jax.nn.sigmoid(ga_ref[...].astype(F32)) * ya)
    o_ref[...] = y.astype(o_ref.dtype)


def _merge(cv, hm, az, w_pc, w_pm, w_pa, gates, d):
    m = cv.shape[0]
    tm = _tile(m, 1024, 16)
    tn = _tile(d, 512, LANES)
    nj = d // tn
    assert gates[1] % tn == 0
    g0 = gates[1] // tn
    rows = lambda a: pl.BlockSpec((tm, a.shape[1]), lambda i, j: (i, 0))
    wcol = lambda w: pl.BlockSpec((w.shape[0], tn), lambda i, j: (0, j))
    gate = lambda b: pl.BlockSpec((tm, tn), lambda i, j, b=b: (i, g0 + b * nj + j))
    return pl.pallas_call(
        _merge_kernel,
        grid=(m // tm, nj),
        in_specs=[rows(cv), rows(hm), rows(az), wcol(w_pc), wcol(w_pm), wcol(w_pa), gate(0), gate(1), gate(2)],
        out_specs=pl.BlockSpec((tm, tn), lambda i, j: (i, j)),
        out_shape=jax.ShapeDtypeStruct((m, d), BF16),
        compiler_params=_params("parallel", "parallel"),
        name="merge",
    )(cv, hm, az, w_pc, w_pm, w_pa, gates[0], gates[0], gates[0])


def _pad_cols(a, width):
    return jnp.pad(a, [(0, 0)] * (a.ndim - 1) + [(0, width - a.shape[-1])])


def _rope_tables(pos, half):
    freqs = ROPE_THETA ** (-jnp.arange(half, dtype=F32) / half)
    ang = pos.astype(F32)[:, None] * freqs
    cos, sin = jnp.cos(ang), jnp.sin(ang)
    return (_pad_cols(jnp.concatenate([cos, cos], axis=-1), ROPE_SLOT),
            _pad_cols(jnp.concatenate([-sin, sin], axis=-1), ROPE_SLOT))


def kernel(x_prompt, x_sample, cache_kv_latent, cache_k_rope, state_conv, state_mlstm_C, state_mlstm_n, state_mlstm_m, ln_g, w_in, b_in, conv_w, conv_b, conv_ln_g, conv_ln_b, w_pc, m_norm_g, w_pm, cq_g, ckv_g, qn_g, qr_g, kn_g, kr_g, w_uq, w_ukv, w_pa, w_out):
    bp, lp, d = x_prompt.shape
    bs, ls, _ = x_sample.shape
    depth = w_in.shape[0]
    past = cache_kv_latent.shape[2]
    kvl = cache_kv_latent.shape[3]
    rope_dim = cache_k_rope.shape[3]
    half = rope_dim // 2
    cw = conv_w.shape[2]
    H, dk, dv = state_mlstm_C.shape[2:]
    mw = H * dv
    ql = cq_g.shape[1]
    nope = qn_g.shape[1]
    heads = w_uq.shape[2] // (nope + rope_dim)
    vd = w_ukv.shape[2] // heads - nope
    aw = heads * vd
    qscale = (nope + rope_dim) ** -0.5 * math.log2(math.e)
    assert rope_dim % 2 == 0 and rope_dim <= ROPE_SLOT and 2 * H <= LANES

    sizes = (cw, cw, cw, H * dk, H * dk, mw, H, H, mw, mw, ql, kvl, rope_dim, aw, N_BRANCH * d)
    offs = [0]
    for s in sizes:
        offs.append(offs[-1] + s)
    o_mi, o_mo, o_ckv, o_az = offs[6], offs[8], offs[11], offs[13]
    assert offs[-1] == w_in.shape[2]
    colA = dict(ca=0, cb=cw, cz=2 * cw, mq=3 * cw, mk=3 * cw + H * dk, mv=3 * cw + 2 * H * dk)
    colB = dict(mo=0, mz=mw, cq=2 * mw)
    colD = dict(ckv=0, kr=kvl, gif=kvl + ROPE_SLOT)
    n_d = kvl + ROPE_SLOT + LANES

    def group_d(w):
        return jnp.concatenate([w[..., o_ckv:o_ckv + kvl], _pad_cols(w[..., o_ckv + kvl:o_az], ROPE_SLOT),
                                _pad_cols(w[..., o_mi:o_mo], LANES)], axis=-1)

    groups = (
        dict(name="prompt", x=x_prompt.reshape(bp * lp, d), nb=bp, seq=lp, past=0,
             pos=jnp.tile(jnp.arange(lp, dtype=jnp.int32), bp)),
        dict(name="sample", x=x_sample.reshape(bs * ls, d), nb=bs, seq=ls, past=past,
             pos=jnp.tile(past + jnp.arange(ls, dtype=jnp.int32), bs)),
    )
    for g in groups:
        g["cos"], g["sin"] = _rope_tables(g["pos"], half)
        g["outs"] = [[] for _ in range(6)]

    for l in range(depth):
        w_a = w_in[l, :, :o_mi].astype(BF16)
        w_b = w_in[l, :, o_mo:o_ckv].astype(BF16)
        w_c = w_in[l, :, o_az:].astype(BF16)
        w_d = group_d(w_in[l]).astype(BF16)
        b_a, b_b, b_c, b_d = b_in[l, :o_mi], b_in[l, o_mo:o_ckv], b_in[l, o_az:], group_d(b_in[l])
        w_uq_slots = _pad_cols(w_uq[l].reshape(ql, heads, nope + rope_dim), nope + ROPE_SLOT
                               ).reshape(ql, heads * (nope + ROPE_SLOT)).astype(BF16)
        qr_g_slot = _pad_cols(qr_g[l], ROPE_SLOT)
        kr_g_slot = _pad_cols(kr_g[l], ROPE_SLOT)
        w_ukv_b = w_ukv[l].astype(BF16)
        w_pc_b, w_pm_b, w_pa_b, w_out_b = (w[l].astype(BF16) for w in (w_pc, w_pm, w_pa, w_out))

        for g in groups:
            nb, seq = g["nb"], g["seq"]
            prompt = g["past"] == 0
            h = _rmsnorm_rows(g["x"], ln_g[l])
            pa = _matmul(h, w_a, bias=b_a, name="in_proj_a")
            pb = _matmul(h, w_b, bias=b_b, name="in_proj_b")
            pc = _matmul(h, w_c, bias=b_c, name="in_proj_c")
            pd = _matmul(h, w_d, bias=b_d, tn=n_d, name="in_proj_d")

            cv, cst = _conv_branch((pa, colA["ca"]), (pa, colA["cb"]), (pa, colA["cz"]), nb, seq,
                                   None if prompt else state_conv[l], conv_w[l], conv_b[l], conv_ln_g[l], conv_ln_b[l])
            state = None if prompt else (state_mlstm_C[l], state_mlstm_n[l], state_mlstm_m[l])
            hm, c_new, n_new, m_new = _mlstm_branch(
                (pa, colA["mq"]), (pa, colA["mk"]), (pa, colA["mv"]), (pd, colD["gif"]), (pb, colB["mo"]), (pb, colB["mz"]),
                nb, seq, state, m_norm_g[l], H, dk, dv)

            qc = _q_proj((pb, colB["cq"]), cq_g[l], w_uq_slots, qn_g[l], qr_g_slot, g["cos"], g["sin"],
                         heads, nope, rope_dim, qscale)
            lat, kr_slot = _latent_norm((pd, colD["ckv"]), (pd, colD["kr"]), ckv_g[l], kr_g_slot, g["cos"], g["sin"], rope_dim)
            kc, vv = _kv_up(lat, kr_slot, w_ukv_b, kn_g[l], heads, nope, vd)
            if prompt:
                az = _attn_prefill(qc, kc, vv, (pc, 0), nb, seq, heads, nope, vd)
            else:
                kc_past, v_past = _kv_up(cache_kv_latent[l].reshape(nb * past, kvl),
                                         _pad_cols(cache_k_rope[l], ROPE_SLOT).reshape(nb * past, ROPE_SLOT),
                                         w_ukv_b, kn_g[l], heads, nope, vd)
                az = _attn_decode(qc, kc_past, v_past, kc, vv, (pc, 0), nb, seq, past, heads, nope, vd)

            merged = _merge(cv, hm, az, w_pc_b, w_pm_b, w_pa_b, (pc, aw), d)
            g["x"] = _matmul(merged, w_out_b, res=g["x"], name="out_proj")
            for lst, val in zip(g["outs"], (cst, c_new, n_new, m_new, lat.reshape(nb, seq, kvl),
                                            kr_slot[:, :rope_dim].reshape(nb, seq, rope_dim))):
                lst.append(val)

    gp, gs = groups
    st = lambda g, k: jnp.stack(g["outs"][k])
    return (gp["x"].reshape(bp, lp, d), gs["x"].reshape(bs, ls, d),
            st(gp, 0), st(gs, 0),
            st(gp, 1), st(gp, 2), st(gp, 3),
            st(gs, 1), st(gs, 2), st(gs, 3),
            st(gp, 4), st(gp, 5), st(gs, 4), st(gs, 5))
```

```python
import functools
import math

import jax
import jax.numpy as jnp
from jax import lax
from jax.experimental import pallas as pl
from jax.experimental.pallas import tpu as pltpu

CHUNK = 64
EPS = 1e-6
ROPE_THETA = 10000.0
N_BRANCH = 3

LANES = 128
SUBLANES = 8
VMEM_LIMIT_BYTES = 56 * 1024 * 1024
ROPE_SLOT = LANES

BF16 = jnp.bfloat16
F32 = jnp.float32


def _tile(n, pref, align):
    t = min(pref, n)
    t -= t % align
    while t >= align:
        if n % t == 0:
            return t
        t -= align
    return n


def _params(*sem):
    return pltpu.CompilerParams(dimension_semantics=sem, vmem_limit_bytes=VMEM_LIMIT_BYTES)


def _rms(x, g, n=None):
    n = x.shape[-1] if n is None else n
    ms = jnp.sum(x * x, axis=-1, keepdims=True) * (1.0 / n)
    return x * lax.rsqrt(ms + EPS) * g


def _silu(x):
    return x * jax.nn.sigmoid(x)


def _col_block(rows, width, off, row_fn):
    assert off % width == 0, (off, width)
    return pl.BlockSpec((rows, width), lambda *g: (row_fn(*g), off // width))


def _rms_kernel(x_ref, g_ref, o_ref):
    o_ref[...] = _rms(x_ref[...], g_ref[...]).astype(o_ref.dtype)


def _rmsnorm_rows(x, g, tm=512):
    m, d = x.shape
    tm = _tile(m, tm, 16)
    return pl.pallas_call(
        _rms_kernel,
        grid=(m // tm,),
        in_specs=[pl.BlockSpec((tm, d), lambda i: (i, 0)), pl.BlockSpec((1, d), lambda i: (0, 0))],
        out_specs=pl.BlockSpec((tm, d), lambda i: (i, 0)),
        out_shape=jax.ShapeDtypeStruct((m, d), BF16),
        compiler_params=_params("parallel"),
        name="rmsnorm",
    )(x, g.reshape(1, d))


def _mm_kernel(*refs, has_bias, has_res, w_rows):
    a_ref, w_ref = refs[0], refs[1]
    o_ref = refs[-1]
    if w_rows:
        acc = lax.dot_general(a_ref[...], w_ref[...].astype(BF16), (((1,), (1,)), ((), ())), preferred_element_type=F32)
    else:
        acc = jnp.dot(a_ref[...], w_ref[...], preferred_element_type=F32)
    k = 2
    if has_bias:
        acc = acc + refs[k][...]
        k += 1
    if has_res:
        acc = acc + refs[k][...]
    o_ref[...] = acc.astype(o_ref.dtype)


def _matmul(a, w, bias=None, res=None, out_dtype=F32, tm=1024, tn=1024, w_rows=None, name="matmul"):
    m, k = a.shape
    tm = _tile(m, tm, 16)
    if w_rows is None:
        n = w.shape[1]
        tn = _tile(n, tn, LANES)
        w_spec = pl.BlockSpec((k, tn), lambda i, j: (0, j))
    else:
        row0, n = w_rows
        tn = _tile(n, tn, LANES)
        row_align = SUBLANES * 4 // w.dtype.itemsize
        assert w.shape[1] == k and row0 % row_align == 0
        w_spec = pl.BlockSpec((pl.Element(tn), pl.Element(k)),
                              lambda i, j: (pl.multiple_of(row0 + j * tn, row_align), 0))
    in_specs = [pl.BlockSpec((tm, k), lambda i, j: (i, 0)), w_spec]
    args = [a, w]
    if bias is not None:
        in_specs.append(pl.BlockSpec((1, tn), lambda i, j: (0, j)))
        args.append(bias.reshape(1, n))
    if res is not None:
        in_specs.append(pl.BlockSpec((tm, tn), lambda i, j: (i, j)))
        args.append(res)
    return pl.pallas_call(
        functools.partial(_mm_kernel, has_bias=bias is not None, has_res=res is not None, w_rows=w_rows is not None),
        grid=(m // tm, n // tn),
        in_specs=in_specs,
        out_specs=pl.BlockSpec((tm, tn), lambda i, j: (i, j)),
        out_shape=jax.ShapeDtypeStruct((m, n), out_dtype),
        compiler_params=_params("parallel", "parallel"),
        name=name,
    )(*args)


def _conv_kernel(*refs, tl, ks, hp, rc, lc, nr, has_state):
    if has_state:
        ca_ref, cb_ref, cz_ref, prev_ref, w_ref, b_ref, lg_ref, lb_ref, cv_ref, st_ref, ubuf, ybuf = refs
    else:
        ca_ref, cb_ref, cz_ref, w_ref, b_ref, lg_ref, lb_ref, cv_ref, st_ref, ubuf, ybuf = refs
    t = pl.program_id(1)
    c = ubuf.shape[1]

    @pl.when(t == 0)
    def _():
        ubuf[0:hp, :] = jnp.zeros((hp, c), F32)
        if has_state:
            ubuf[hp - ks:hp, :] = prev_ref[0]

    @pl.when(t > 0)
    def _():
        ubuf[0:hp, :] = ubuf[tl:tl + hp, :]

    ubuf[hp:hp + tl, :] = ca_ref[...] * jax.nn.sigmoid(cb_ref[...])

    for c0 in range(0, c, lc):
        cols = slice(c0, c0 + lc)
        bias = b_ref[:, cols]
        for r in range(0, tl, rc):
            acc = jnp.broadcast_to(bias, (rc, lc))
            for s in range(SUBLANES):
                taps = [j for j in range(ks + 1) if (hp - ks + j) % SUBLANES == s]
                if not taps:
                    continue
                ext = rc + (SUBLANES if s else 0)
                part = None
                for j in taps:
                    a0 = r + (hp - ks + j) - s
                    term = (ubuf[a0:a0 + ext, cols].reshape(ext // SUBLANES, SUBLANES, lc) * w_ref[j, :, cols]
                            ).reshape(ext, lc)
                    part = term if part is None else part + term
                acc = acc + (part[s:s + rc, :] if s else part)
            ybuf[r:r + rc, cols] = acc

    lg = lg_ref[...]
    lb = lb_ref[...]
    for r in range(0, tl, nr):
        rows = slice(r, r + nr)
        conv = ybuf[rows, :]
        mu = jnp.mean(conv, axis=-1, keepdims=True)
        d = conv - mu
        var = jnp.mean(d * d, axis=-1, keepdims=True)
        y = d * lax.rsqrt(var + EPS) * lg + lb
        cv_ref[rows, :] = (_silu(y) * _silu(cz_ref[rows, :])).astype(cv_ref.dtype)

    @pl.when(t == pl.num_programs(1) - 1)
    def _():
        st_ref[0] = ubuf[hp + tl - ks:hp + tl, :]


def _conv_branch(ca, cb, cz, nb, seq, prev, w, b, lg, lb):
    kk, c = w.shape
    ks = kk - 1
    hp = -(-ks // SUBLANES) * SUBLANES
    tl = _tile(seq, 128, 32)
    rc, lc, nr = tl, LANES, 16
    assert tl % rc == 0 and tl % nr == 0 and tl >= hp and seq >= ks
    nt = seq // tl
    row = lambda bi, ti: bi * nt + ti
    vec_spec = pl.BlockSpec((1, c), lambda bi, ti: (0, 0))
    in_specs = [_col_block(tl, c, ca[1], row), _col_block(tl, c, cb[1], row), _col_block(tl, c, cz[1], row)]
    args = [ca[0], cb[0], cz[0]]
    if prev is not None:
        in_specs.append(pl.BlockSpec((1, ks, c), lambda bi, ti: (bi, 0, 0)))
        args.append(prev)
    in_specs += [pl.BlockSpec((kk, SUBLANES, c), lambda bi, ti: (0, 0, 0)), vec_spec, vec_spec, vec_spec]
    args += [jnp.broadcast_to(w[:, None, :], (kk, SUBLANES, c)), b.reshape(1, c), lg.reshape(1, c), lb.reshape(1, c)]
    return pl.pallas_call(
        functools.partial(_conv_kernel, tl=tl, ks=ks, hp=hp, rc=rc, lc=lc, nr=nr, has_state=prev is not None),
        grid=(nb, nt),
        in_specs=in_specs,
        out_specs=[pl.BlockSpec((tl, c), lambda bi, ti: (bi * nt + ti, 0)),
                   pl.BlockSpec((1, ks, c), lambda bi, ti: (bi, 0, 0))],
        out_shape=[jax.ShapeDtypeStruct((nb * seq, c), BF16), jax.ShapeDtypeStruct((nb, ks, c), F32)],
        scratch_shapes=[pltpu.VMEM((hp + tl, c), F32), pltpu.VMEM((tl, c), F32)],
        compiler_params=_params("parallel", "arbitrary"),
        name="conv_branch",
    )(*args)


def _mlstm_kernel(*refs, L, H, dk, dv, has_state):
    if has_state:
        (q_ref, k_ref, v_ref, g_ref, o_ref, z_ref, ng_ref, c0_ref, n0_ref, m0_ref,
         hm_ref, cs_ref, ns_ref, ms_ref) = refs
    else:
        q_ref, k_ref, v_ref, g_ref, o_ref, z_ref, ng_ref, hm_ref, cs_ref, ns_ref, ms_ref = refs

    @pl.when(pl.program_id(1) == 0)
    def _():
        if has_state:
            cs_ref[...] = c0_ref[...]
            ns_ref[...] = n0_ref[...]
            ms_ref[...] = m0_ref[...]
        else:
            cs_ref[...] = jnp.zeros(cs_ref.shape, F32)
            ns_ref[...] = jnp.zeros(ns_ref.shape, F32)
            ms_ref[...] = jnp.zeros(ms_ref.shape, F32)

    row = lax.broadcasted_iota(jnp.int32, (L, L), 0)
    col = lax.broadcasted_iota(jnp.int32, (L, L), 1)
    eye = row == col
    tril = col <= row
    gates = g_ref[...]
    glane = lax.broadcasted_iota(jnp.int32, gates.shape, 1)
    neg_inf = jnp.float32(-jnp.inf)

    for h in range(H):
        li_col = jnp.sum(jnp.where(glane == h, gates, 0.0), axis=1, keepdims=True)
        f_col = jnp.sum(jnp.where(glane == H + h, gates, 0.0), axis=1, keepdims=True)
        lf_col = jnp.minimum(f_col, 0.0) - jnp.log(1.0 + jnp.exp(-jnp.abs(f_col)))
        lf_row = jnp.sum(jnp.where(eye, lf_col, 0.0), axis=0, keepdims=True)
        li_row = jnp.sum(jnp.where(eye, li_col, 0.0), axis=0, keepdims=True)
        bt_col = jnp.sum(jnp.where(tril, lf_row, 0.0), axis=1, keepdims=True)
        bt_row = jnp.sum(jnp.where(row <= col, lf_col, 0.0), axis=0, keepdims=True)

        m_prev = ms_ref[0, 0:1, h:h + 1]
        c_prev = cs_ref[0, h]
        n_prev = ns_ref[0, h:h + 1, :]

        q = q_ref[:, h * dk:(h + 1) * dk]
        k = k_ref[:, h * dk:(h + 1) * dk] * (dk ** -0.5)
        v = v_ref[:, h * dv:(h + 1) * dv]
        qb = q.astype(BF16)
        kb = k.astype(BF16)
        vb = v.astype(BF16)

        logw = jnp.where(tril, bt_col - bt_row + li_row, neg_inf)
        a_col = bt_col + m_prev
        m_t = jnp.maximum(a_col, jnp.max(logw, axis=1, keepdims=True))
        w_inter = jnp.exp(a_col - m_t)
        qk = lax.dot_general(qb, kb, (((1,), (1,)), ((), ())), preferred_element_type=F32)
        sw = jnp.exp(logw - m_t) * qk
        num = (jnp.dot(sw.astype(BF16), vb, preferred_element_type=F32)
               + jnp.dot(qb, c_prev.astype(BF16), preferred_element_type=F32) * w_inter)
        den = jnp.sum(sw, axis=1, keepdims=True) + w_inter * jnp.sum(q * n_prev, axis=1, keepdims=True)
        denom = jnp.maximum(jnp.abs(den), jnp.exp(-m_t))
        hh = num / denom

        b_last = bt_col[L - 1:L, :]
        gs_col = b_last - bt_col + li_col
        m_new = jnp.maximum(b_last + m_prev, jnp.max(gs_col, axis=0, keepdims=True))
        ws_col = jnp.exp(gs_col - m_new)
        decay = jnp.exp(b_last + m_prev - m_new)
        kw = ws_col * k
        cs_ref[0, h] = decay * c_prev + lax.dot_general(
            kw.astype(BF16), vb, (((0,), (0,)), ((), ())), preferred_element_type=F32)
        ns_ref[0, h:h + 1, :] = decay * n_prev + jnp.sum(kw, axis=0, keepdims=True)
        ms_ref[0, 0:1, h:h + 1] = m_new

        sl = slice(h * dv, (h + 1) * dv)
        hn = _rms(hh, ng_ref[:, sl])
        hm_ref[:, sl] = (hn * jax.nn.sigmoid(o_ref[:, sl]) * _silu(z_ref[:, sl])).astype(hm_ref.dtype)


def _mlstm_branch(mq, mk, mv, mif, mo, mz, nb, seq, state, norm_g, H, dk, dv):
    L = min(seq, CHUNK)
    assert seq % L == 0
    nc = seq // L
    row = lambda bi, ci: bi * nc + ci
    ops = ((mq, H * dk), (mk, H * dk), (mv, H * dv), (mif, LANES), (mo, H * dv), (mz, H * dv))
    in_specs = [_col_block(L, width, op[1], row) for op, width in ops]
    in_specs.append(pl.BlockSpec((1, H * dv), lambda bi, ci: (0, 0)))
    args = [op[0] for op, _ in ops] + [norm_g.reshape(1, H * dv)]
    st_specs = [pl.BlockSpec((1, H, dk, dv), lambda bi, ci: (bi, 0, 0, 0)),
                pl.BlockSpec((1, H, dk), lambda bi, ci: (bi, 0, 0)),
                pl.BlockSpec((1, 1, H), lambda bi, ci: (bi, 0, 0))]
    if state is not None:
        c0, n0, m0 = state
        in_specs += st_specs
        args += [c0, n0, m0.reshape(nb, 1, H)]
    hm, cs, ns, ms = pl.pallas_call(
        functools.partial(_mlstm_kernel, L=L, H=H, dk=dk, dv=dv, has_state=state is not None),
        grid=(nb, nc),
        in_specs=in_specs,
        out_specs=[pl.BlockSpec((L, H * dv), lambda bi, ci: (bi * nc + ci, 0))] + st_specs,
        out_shape=[jax.ShapeDtypeStruct((nb * seq, H * dv), BF16),
                   jax.ShapeDtypeStruct((nb, H, dk, dv), F32),
                   jax.ShapeDtypeStruct((nb, H, dk), F32),
                   jax.ShapeDtypeStruct((nb, 1, H), F32)],
        compiler_params=_params("parallel", "arbitrary"),
        name="mlstm_branch",
    )(*args)
    return hm, cs, ns, ms.reshape(nb, H)


def _rope_slot(x, cos, sin, half):
    swapped = pltpu.roll(x, ROPE_SLOT - half, 1) + pltpu.roll(x, half, 1)
    return x * cos + swapped * sin


def _q_kernel(a_ref, cg_ref, w_ref, qn_ref, qr_ref, cos_ref, sin_ref, o_ref, *, hg, nope, rope_dim, scale):
    hq = _rms(a_ref[...], cg_ref[...]).astype(BF16)
    cos = cos_ref[...]
    sin = sin_ref[...]
    slot = nope + ROPE_SLOT
    for h in range(hg):
        b0 = h * slot
        qa = jnp.dot(hq, w_ref[:, b0:b0 + slot], preferred_element_type=F32)
        xn = _rms(qa[:, :nope], qn_ref[...])
        xr = _rope_slot(_rms(qa[:, nope:], qr_ref[...], rope_dim), cos, sin, rope_dim // 2)
        o_ref[:, b0:b0 + nope] = (xn * scale).astype(o_ref.dtype)
        o_ref[:, b0 + nope:b0 + slot] = (xr * scale).astype(o_ref.dtype)


def _q_proj(acq, cq_g, w_uq_slots, qn_g, qr_g_slot, cos, sin, heads, nope, rope_dim, scale):
    m = acq[0].shape[0]
    ql = cq_g.shape[0]
    slot = nope + ROPE_SLOT
    hg = _tile(heads, 4, 1)
    tm = _tile(m, 512, 16)
    vec = lambda n: pl.BlockSpec((1, n), lambda i, j: (0, 0))
    return pl.pallas_call(
        functools.partial(_q_kernel, hg=hg, nope=nope, rope_dim=rope_dim, scale=scale),
        grid=(m // tm, heads // hg),
        in_specs=[_col_block(tm, ql, acq[1], lambda i, j: i), vec(ql),
                  pl.BlockSpec((ql, hg * slot), lambda i, j: (0, j)), vec(nope), vec(ROPE_SLOT),
                  pl.BlockSpec((tm, ROPE_SLOT), lambda i, j: (i, 0)), pl.BlockSpec((tm, ROPE_SLOT), lambda i, j: (i, 0))],
        out_specs=pl.BlockSpec((tm, hg * slot), lambda i, j: (i, j)),
        out_shape=jax.ShapeDtypeStruct((m, heads * slot), BF16),
        compiler_params=_params("parallel", "parallel"),
        name="q_proj",
    )(acq[0], cq_g.reshape(1, ql), w_uq_slots, qn_g.reshape(1, nope), qr_g_slot.reshape(1, ROPE_SLOT), cos, sin)


def _latent_kernel(c_ref, r_ref, cg_ref, rg_ref, cos_ref, sin_ref, lat_ref, kr_ref, *, rope_dim):
    lat_ref[...] = _rms(c_ref[...], cg_ref[...])
    kr_ref[...] = _rope_slot(_rms(r_ref[...], rg_ref[...], rope_dim), cos_ref[...], sin_ref[...], rope_dim // 2)


def _latent_norm(ackv, akr, ckv_g, kr_g_slot, cos, sin, rope_dim):
    m = ackv[0].shape[0]
    kvl = ckv_g.shape[0]
    tm = _tile(m, 1024, 8)
    rows = lambda n: pl.BlockSpec((tm, n), lambda i: (i, 0))
    return pl.pallas_call(
        functools.partial(_latent_kernel, rope_dim=rope_dim),
        grid=(m // tm,),
        in_specs=[_col_block(tm, kvl, ackv[1], lambda i: i), _col_block(tm, ROPE_SLOT, akr[1], lambda i: i),
                  pl.BlockSpec((1, kvl), lambda i: (0, 0)), pl.BlockSpec((1, ROPE_SLOT), lambda i: (0, 0)),
                  rows(ROPE_SLOT), rows(ROPE_SLOT)],
        out_specs=[rows(kvl), rows(ROPE_SLOT)],
        out_shape=[jax.ShapeDtypeStruct((m, kvl), F32), jax.ShapeDtypeStruct((m, ROPE_SLOT), F32)],
        compiler_params=_params("parallel"),
        name="latent_norm",
    )(ackv[0], akr[0], ckv_g.reshape(1, kvl), kr_g_slot.reshape(1, ROPE_SLOT), cos, sin)


def _kv_kernel(lat_ref, kr_ref, w_ref, kn_ref, kc_ref, v_ref, *, hg, nope, vd):
    lat = lat_ref[...].astype(BF16)
    kr = kr_ref[...].astype(kc_ref.dtype)
    slot = nope + ROPE_SLOT
    for h in range(hg):
        b0 = h * (nope + vd)
        kv = jnp.dot(lat, w_ref[:, b0:b0 + nope + vd], preferred_element_type=F32)
        kc_ref[:, h * slot:h * slot + nope] = _rms(kv[:, :nope], kn_ref[...]).astype(kc_ref.dtype)
        kc_ref[:, h * slot + nope:(h + 1) * slot] = kr
        v_ref[:, h * vd:(h + 1) * vd] = kv[:, nope:].astype(v_ref.dtype)


def _kv_up(lat, kr_slot, w_ukv, kn_g, heads, nope, vd, row0=0, rows=None):
    kvl = lat.shape[1]
    rows = lat.shape[0] if rows is None else rows
    slot = nope + ROPE_SLOT
    hg = _tile(heads, 4, 1)
    tm = _tile(rows, 1024, 16)
    assert row0 % tm == 0
    i0 = row0 // tm
    return pl.pallas_call(
        functools.partial(_kv_kernel, hg=hg, nope=nope, vd=vd),
        grid=(rows // tm, heads // hg),
        in_specs=[pl.BlockSpec((tm, kvl), lambda i, j: (i0 + i, 0)), pl.BlockSpec((tm, ROPE_SLOT), lambda i, j: (i0 + i, 0)),
                  pl.BlockSpec((kvl, hg * (nope + vd)), lambda i, j: (0, j)), pl.BlockSpec((1, nope), lambda i, j: (0, 0))],
        out_specs=[pl.BlockSpec((tm, hg * slot), lambda i, j: (i, j)), pl.BlockSpec((tm, hg * vd), lambda i, j: (i, j))],
        out_shape=[jax.ShapeDtypeStruct((rows, heads * slot), BF16), jax.ShapeDtypeStruct((rows, heads * vd), BF16)],
        compiler_params=_params("parallel", "parallel"),
        name="kv_up",
    )(lat, kr_slot, w_ukv, kn_g.reshape(1, nope))


def _scores(q, k):
    return lax.dot_general(q, k, (((1,), (1,)), ((), ())), preferred_element_type=F32)


def _softmax_step(s, v, m, l, acc, mask):
    if mask is not None:
        s = jnp.where(mask, s, -jnp.inf)
    m_new = jnp.maximum(m, jnp.max(s, axis=1, keepdims=True))
    alpha = jnp.exp2(m - m_new)
    p = jnp.exp2(s - m_new)
    l = alpha * l + jnp.sum(p, axis=1, keepdims=True)
    acc = alpha * acc + jnp.dot(p.astype(v.dtype), v, preferred_element_type=F32)
    return m_new, l, acc


def _softmax_init(rows, vd):
    return (jnp.full((rows, 1), -jnp.inf, F32), jnp.zeros((rows, 1), F32), jnp.zeros((rows, vd), F32))


def _attn_prefill_kernel(q_ref, k_ref, v_ref, z_ref, o_ref, *, tq, vd):
    i = pl.program_id(2)
    q = q_ref[...]
    ktile = lambda j: k_ref[pl.ds(pl.multiple_of(j * tq, tq), tq), :]
    vtile = lambda j: v_ref[pl.ds(pl.multiple_of(j * tq, tq), tq), :]

    def body(j, carry):
        state, s_cur = carry
        s_next = _scores(q, ktile(j + 1))
        return _softmax_step(s_cur, vtile(j), *state, None), s_next

    state, s_diag = lax.fori_loop(0, i, body, (_softmax_init(tq, vd), _scores(q, ktile(0))))
    rchunk = lax.broadcasted_iota(jnp.int32, (tq, tq), 0) // CHUNK
    cchunk = lax.broadcasted_iota(jnp.int32, (tq, tq), 1) // CHUNK
    _, l, acc = _softmax_step(s_diag, vtile(i), *state, cchunk <= rchunk)
    o_ref[...] = (acc / l * _silu(z_ref[...].astype(F32))).astype(o_ref.dtype)


def _attn_prefill(qc, kc, vv, az, nb, seq, heads, nope, vd):
    slot = nope + ROPE_SLOT
    tq = _tile(seq, 512, CHUNK)
    nq = seq // tq
    return pl.pallas_call(
        functools.partial(_attn_prefill_kernel, tq=tq, vd=vd),
        grid=(nb, heads, nq),
        in_specs=[pl.BlockSpec((tq, slot), lambda b, h, i: (b * nq + i, h)),
                  pl.BlockSpec((seq, slot), lambda b, h, i: (b, h)),
                  pl.BlockSpec((seq, vd), lambda b, h, i: (b, h)),
                  pl.BlockSpec((tq, vd), lambda b, h, i: (b * nq + i, az[1] // vd + h))],
        out_specs=pl.BlockSpec((tq, vd), lambda b, h, i: (b * nq + i, h)),
        out_shape=jax.ShapeDtypeStruct((nb * seq, heads * vd), BF16),
        compiler_params=_params("parallel", "parallel", "arbitrary"),
        name="attn_prefill",
    )(qc, kc, vv, az[0])


def _attn_decode_kernel(q_ref, kp_ref, vp_ref, kn_ref, vn_ref, z_ref, o_ref, *, past, vd, slot, hg):
    lq = q_ref.shape[0]

    def mask(t, base):
        qchunk = (past + lax.broadcasted_iota(jnp.int32, (lq, t), 0)) // CHUNK
        kchunk = (base + lax.broadcasted_iota(jnp.int32, (lq, t), 1)) // CHUNK
        return kchunk <= qchunk

    mask_past = mask(past, 0)
    mask_new = mask(lq, past)
    for g in range(hg):
        q = q_ref[:, g * slot:(g + 1) * slot]
        ks = slice(g * slot, (g + 1) * slot)
        vs = slice(g * vd, (g + 1) * vd)
        carry = _softmax_step(_scores(q, kp_ref[:, ks]), vp_ref[:, vs], *_softmax_init(lq, vd), mask_past)
        _, l, acc = _softmax_step(_scores(q, kn_ref[:, ks]), vn_ref[:, vs], *carry, mask_new)
        o_ref[:, vs] = (acc / l * _silu(z_ref[:, vs].astype(F32))).astype(o_ref.dtype)


def _attn_decode(qc, kc_past, v_past, kc_new, v_new, az, nb, seq, past, heads, nope, vd):
    slot = nope + ROPE_SLOT
    hg = _tile(heads, 4, 1)
    assert past > 0
    return pl.pallas_call(
        functools.partial(_attn_decode_kernel, past=past, vd=vd, slot=slot, hg=hg),
        grid=(nb, heads // hg),
        in_specs=[pl.BlockSpec((seq, hg * slot), lambda b, h: (b, h)),
                  pl.BlockSpec((past, hg * slot), lambda b, h: (b, h)),
                  pl.BlockSpec((past, hg * vd), lambda b, h: (b, h)),
                  pl.BlockSpec((seq, hg * slot), lambda b, h: (b, h)),
                  pl.BlockSpec((seq, hg * vd), lambda b, h: (b, h)),
                  pl.BlockSpec((seq, hg * vd), lambda b, h: (b, az[1] // (hg * vd) + h))],
        out_specs=pl.BlockSpec((seq, hg * vd), lambda b, h: (b, h)),
        out_shape=jax.ShapeDtypeStruct((nb * seq, heads * vd), BF16),
        compiler_params=_params("parallel", "parallel"),
        name="attn_decode",
    )(qc, kc_past, v_past, kc_new, v_new, az[0])


def _merge_kernel(cv_ref, hm_ref, az_ref, wc_ref, wm_ref, wa_ref, gc_ref, gm_ref, ga_ref, o_ref):
    yc = jnp.dot(cv_ref[...], wc_ref[...], preferred_element_type=F32)
    ym = jnp.dot(hm_ref[...], wm_ref[...], preferred_element_type=F32)
    ya = jnp.dot(az_ref[...], wa_ref[...], preferred_element_type=F32)
    y = (jax.nn.sigmoid(gc_ref[...].astype(F32)) * yc + jax.nn.sigmoid(gm_ref[...].astype(F32)) * ym
         + jax.nn.sigmoid(ga_ref[...].astype(F32)) * ya)
    o_ref[...] = y.astype(o_ref.dtype)


def _merge(cv, hm, az, w_pc, w_pm, w_pa, gates, d):
    m = cv.shape[0]
    tm = _tile(m, 1024, 16)
    tn = _tile(d, 1024, LANES)
    nj = d // tn
    assert gates[1] % tn == 0
    g0 = gates[1] // tn
    rows = lambda a: pl.BlockSpec((tm, a.shape[1]), lambda i, j: (i, 0))
    wcol = lambda w: pl.BlockSpec((w.shape[0], tn), lambda i, j: (0, j))
    gate = lambda b: pl.BlockSpec((tm, tn), lambda i, j, b=b: (i, g0 + b * nj + j))
    return pl.pallas_call(
        _merge_kernel,
        grid=(m // tm, nj),
        in_specs=[rows(cv), rows(hm), rows(az), wcol(w_pc), wcol(w_pm), wcol(w_pa), gate(0), gate(1), gate(2)],
        out_specs=pl.BlockSpec((tm, tn), lambda i, j: (i, j)),
        out_shape=jax.ShapeDtypeStruct((m, d), BF16),
        compiler_params=_params("parallel", "parallel"),
        name="merge",
    )(cv, hm, az, w_pc, w_pm, w_pa, gates[0], gates[0], gates[0])


def _pad_cols(a, width):
    return jnp.pad(a, [(0, 0)] * (a.ndim - 1) + [(0, width - a.shape[-1])])


def _rope_tables(pos, half):
    freqs = ROPE_THETA ** (-jnp.arange(half, dtype=F32) / half)
    ang = pos.astype(F32)[:, None] * freqs
    cos, sin = jnp.cos(ang), jnp.sin(ang)
    return (_pad_cols(jnp.concatenate([cos, cos], axis=-1), ROPE_SLOT),
            _pad_cols(jnp.concatenate([-sin, sin], axis=-1), ROPE_SLOT))


def kernel(x_prompt, x_sample, cache_kv_latent, cache_k_rope, state_conv, state_mlstm_C, state_mlstm_n, state_mlstm_m, ln_g, w_in, b_in, conv_w, conv_b, conv_ln_g, conv_ln_b, w_pc, m_norm_g, w_pm, cq_g, ckv_g, qn_g, qr_g, kn_g, kr_g, w_uq, w_ukv, w_pa, w_out):
    bp, lp, d = x_prompt.shape
    bs, ls, _ = x_sample.shape
    depth = w_in.shape[0]
    past = cache_kv_latent.shape[2]
    kvl = cache_kv_latent.shape[3]
    rope_dim = cache_k_rope.shape[3]
    half = rope_dim // 2
    cw = conv_w.shape[2]
    H, dk, dv = state_mlstm_C.shape[2:]
    mw = H * dv
    ql = cq_g.shape[1]
    nope = qn_g.shape[1]
    heads = w_uq.shape[2] // (nope + rope_dim)
    vd = w_ukv.shape[2] // heads - nope
    aw = heads * vd
    qscale = (nope + rope_dim) ** -0.5 * math.log2(math.e)
    assert rope_dim % 2 == 0 and rope_dim <= ROPE_SLOT and 2 * H <= LANES

    sizes = (cw, cw, cw, H * dk, H * dk, mw, H, H, mw, mw, ql, kvl, rope_dim, aw, N_BRANCH * d)
    offs = [0]
    for s in sizes:
        offs.append(offs[-1] + s)
    o_mi, o_mo, o_ckv, o_az = offs[6], offs[8], offs[11], offs[13]
    assert offs[-1] == w_in.shape[2]
    colA = dict(ca=0, cb=cw, cz=2 * cw, mq=3 * cw, mk=3 * cw + H * dk, mv=3 * cw + 2 * H * dk)
    colB = dict(mo=0, mz=mw, cq=2 * mw)
    colD = dict(ckv=0, kr=kvl, gif=kvl + ROPE_SLOT)
    n_d = kvl + ROPE_SLOT + LANES

    def group_d(w, axis):
        def cols(a, b, width):
            pad = [(0, 0)] * w.ndim
            pad[axis] = (0, width - (b - a))
            return jnp.pad(lax.slice_in_dim(w, a, b, axis=axis), pad)
        return jnp.concatenate([cols(o_ckv, o_ckv + kvl, kvl), cols(o_ckv + kvl, o_az, ROPE_SLOT),
                                cols(o_mi, o_mo, LANES)], axis=axis)

    groups = (
        dict(name="prompt", x=x_prompt.reshape(bp * lp, d), nb=bp, seq=lp, past=0,
             pos=jnp.tile(jnp.arange(lp, dtype=jnp.int32), bp)),
        dict(name="sample", x=x_sample.reshape(bs * ls, d), nb=bs, seq=ls, past=past,
             pos=jnp.tile(past + jnp.arange(ls, dtype=jnp.int32), bs)),
    )
    for g in groups:
        g["cos"], g["sin"] = _rope_tables(g["pos"], half)
        g["outs"] = [[] for _ in range(6)]

    cache_lat_rows = cache_kv_latent.reshape(depth * bs * past, kvl)
    cache_kr_rows = _pad_cols(cache_k_rope, ROPE_SLOT).reshape(depth * bs * past, ROPE_SLOT)
    n_in = w_in.shape[2]
    w_in_rows = jnp.swapaxes(w_in, 1, 2).reshape(depth * n_in, d)
    for l in range(depth):
        w_d = group_d(lax.slice_in_dim(w_in_rows, l * n_in, (l + 1) * n_in, axis=0), 0).astype(BF16)
        b_a, b_b, b_c, b_d = b_in[l, :o_mi], b_in[l, o_mo:o_ckv], b_in[l, o_az:], group_d(b_in[l], 0)
        w_uq_slots = _pad_cols(w_uq[l].reshape(ql, heads, nope + rope_dim), nope + ROPE_SLOT
                               ).reshape(ql, heads * (nope + ROPE_SLOT)).astype(BF16)
        qr_g_slot = _pad_cols(qr_g[l], ROPE_SLOT)
        kr_g_slot = _pad_cols(kr_g[l], ROPE_SLOT)
        w_ukv_b = w_ukv[l].astype(BF16)
        w_pc_b, w_pm_b, w_pa_b, w_out_b = (w[l].astype(BF16) for w in (w_pc, w_pm, w_pa, w_out))

        for g in groups:
            nb, seq = g["nb"], g["seq"]
            prompt = g["past"] == 0
            h = _rmsnorm_rows(g["x"], ln_g[l])
            big = dict(tm=2048, tn=256)
            pa = _matmul(h, w_in_rows, bias=b_a, w_rows=(l * n_in, o_mi), name="in_proj_a", **big)
            pb = _matmul(h, w_in_rows, bias=b_b, w_rows=(l * n_in + o_mo, o_ckv - o_mo), name="in_proj_b", **big)
            pc = _matmul(h, w_in_rows, bias=b_c, w_rows=(l * n_in + o_az, n_in - o_az), out_dtype=BF16,
                         name="in_proj_c", **big)
            pd = _matmul(h, w_d, bias=b_d, tn=n_d, w_rows=(0, n_d), name="in_proj_d")

            cv, cst = _conv_branch((pa, colA["ca"]), (pa, colA["cb"]), (pa, colA["cz"]), nb, seq,
                                   None if prompt else state_conv[l], conv_w[l], conv_b[l], conv_ln_g[l], conv_ln_b[l])
            state = None if prompt else (state_mlstm_C[l], state_mlstm_n[l], state_mlstm_m[l])
            hm, c_new, n_new, m_new = _mlstm_branch(
                (pa, colA["mq"]), (pa, colA["mk"]), (pa, colA["mv"]), (pd, colD["gif"]), (pb, colB["mo"]), (pb, colB["mz"]),
                nb, seq, state, m_norm_g[l], H, dk, dv)

            qc = _q_proj((pb, colB["cq"]), cq_g[l], w_uq_slots, qn_g[l], qr_g_slot, g["cos"], g["sin"],
                         heads, nope, rope_dim, qscale)
            lat, kr_slot = _latent_norm((pd, colD["ckv"]), (pd, colD["kr"]), ckv_g[l], kr_g_slot, g["cos"], g["sin"], rope_dim)
            kc, vv = _kv_up(lat, kr_slot, w_ukv_b, kn_g[l], heads, nope, vd)
            if prompt:
                az = _attn_prefill(qc, kc, vv, (pc, 0), nb, seq, heads, nope, vd)
            else:
                kc_past, v_past = _kv_up(cache_lat_rows, cache_kr_rows, w_ukv_b, kn_g[l], heads, nope, vd,
                                         row0=l * nb * past, rows=nb * past)
                az = _attn_decode(qc, kc_past, v_past, kc, vv, (pc, 0), nb, seq, past, heads, nope, vd)

            merged = _merge(cv, hm, az, w_pc_b, w_pm_b, w_pa_b, (pc, aw), d)
            g["x"] = _matmul(merged, w_out_b, res=g["x"], name="out_proj")
            for lst, val in zip(g["outs"], (cst, c_new, n_new, m_new, lat.reshape(nb, seq, kvl),
                                            kr_slot[:, :rope_dim].reshape(nb, seq, rope_dim))):
                lst.append(val)

    gp, gs = groups
    st = lambda g, k: jnp.stack(g["outs"][k])
    return (gp["x"].reshape(bp, lp, d), gs["x"].reshape(bs, ls, d),
            st(gp, 0), st(gs, 0),
            st(gp, 1), st(gp, 2), st(gp, 3),
            st(gs, 1), st(gs, 2), st(gs, 3),
            st(gp, 4), st(gp, 5), st(gs, 4), st(gs, 5))
```

```python
import functools
import math

import jax
import jax.numpy as jnp
from jax import lax
from jax.experimental import pallas as pl
from jax.experimental.pallas import tpu as pltpu

CHUNK = 64
EPS = 1e-6
ROPE_THETA = 10000.0
N_BRANCH = 3

LANES = 128
SUBLANES = 8
MXU_WIDTH = 256
VMEM_LIMIT_BYTES = 56 * 1024 * 1024
ROPE_SLOT = LANES

BF16 = jnp.bfloat16
F32 = jnp.float32


def _tile(n, pref, align):
    t = min(pref, n)
    t -= t % align
    while t >= align:
        if n % t == 0:
            return t
        t -= align
    return n


def _params(*sem):
    return pltpu.CompilerParams(dimension_semantics=sem, vmem_limit_bytes=VMEM_LIMIT_BYTES)


def _rms(x, g, n=None):
    n = x.shape[-1] if n is None else n
    ms = jnp.sum(x * x, axis=-1, keepdims=True) * (1.0 / n)
    return x * lax.rsqrt(ms + EPS) * g


def _silu(x):
    return x * jax.nn.sigmoid(x)


def _col_block(rows, width, off, row_fn):
    assert off % width == 0, (off, width)
    return pl.BlockSpec((rows, width), lambda *g: (row_fn(*g), off // width))


def _rms_kernel(x_ref, g_ref, o_ref):
    o_ref[...] = _rms(x_ref[...], g_ref[...]).astype(o_ref.dtype)


def _rmsnorm_rows(x, g, tm=512):
    m, d = x.shape
    tm = _tile(m, tm, 16)
    return pl.pallas_call(
        _rms_kernel,
        grid=(m // tm,),
        in_specs=[pl.BlockSpec((tm, d), lambda i: (i, 0)), pl.BlockSpec((1, d), lambda i: (0, 0))],
        out_specs=pl.BlockSpec((tm, d), lambda i: (i, 0)),
        out_shape=jax.ShapeDtypeStruct((m, d), BF16),
        compiler_params=_params("parallel"),
        name="rmsnorm",
    )(x, g.reshape(1, d))


def _cast_kernel(w_ref, o_ref):
    o_ref[...] = w_ref[...].astype(o_ref.dtype)


def _cast_rows(w, row0, n, tr=512):
    k = w.shape[1]
    tr = _tile(n, tr, 16)
    assert row0 % SUBLANES == 0 and w.dtype == F32
    return pl.pallas_call(
        _cast_kernel,
        grid=(n // tr,),
        in_specs=[pl.BlockSpec((pl.Element(tr), pl.Element(k)), lambda i: (pl.multiple_of(row0 + i * tr, SUBLANES), 0))],
        out_specs=pl.BlockSpec((tr, k), lambda i: (i, 0)),
        out_shape=jax.ShapeDtypeStruct((n, k), BF16),
        compiler_params=_params("parallel"),
        name="cast_rows",
    )(w)


def _mm_kernel(*refs, has_bias, has_res, w_rows):
    a_ref, w_ref = refs[0], refs[1]
    o_ref = refs[-1]
    if w_rows:
        acc = lax.dot_general(a_ref[...], w_ref[...].astype(BF16), (((1,), (1,)), ((), ())), preferred_element_type=F32)
    else:
        acc = jnp.dot(a_ref[...], w_ref[...], preferred_element_type=F32)
    k = 2
    if has_bias:
        acc = acc + refs[k][...]
        k += 1
    if has_res:
        acc = acc + refs[k][...]
    o_ref[...] = acc.astype(o_ref.dtype)


def _matmul(a, w, bias=None, res=None, out_dtype=F32, tm=1024, tn=1024, w_rows=None, name="matmul"):
    m, k = a.shape
    tm = _tile(m, tm, 16)
    if w_rows is None:
        n = w.shape[1]
        tn = _tile(n, tn, LANES)
        w_spec = pl.BlockSpec((k, tn), lambda i, j: (0, j))
    else:
        row0, n = w_rows
        tn = _tile(n, tn, LANES)
        row_align = SUBLANES * 4 // w.dtype.itemsize
        assert w.shape[1] == k and row0 % row_align == 0
        w_spec = pl.BlockSpec((pl.Element(tn), pl.Element(k)),
                              lambda i, j: (pl.multiple_of(row0 + j * tn, row_align), 0))
    in_specs = [pl.BlockSpec((tm, k), lambda i, j: (i, 0)), w_spec]
    args = [a, w]
    if bias is not None:
        in_specs.append(pl.BlockSpec((1, tn), lambda i, j: (0, j)))
        args.append(bias.reshape(1, n))
    if res is not None:
        in_specs.append(pl.BlockSpec((tm, tn), lambda i, j: (i, j)))
        args.append(res)
    return pl.pallas_call(
        functools.partial(_mm_kernel, has_bias=bias is not None, has_res=res is not None, w_rows=w_rows is not None),
        grid=(m // tm, n // tn),
        in_specs=in_specs,
        out_specs=pl.BlockSpec((tm, tn), lambda i, j: (i, j)),
        out_shape=jax.ShapeDtypeStruct((m, n), out_dtype),
        compiler_params=_params("parallel", "parallel"),
        name=name,
    )(*args)


def _conv_kernel(*refs, tl, ks, hp, rc, lc, nr, has_state):
    if has_state:
        ca_ref, cb_ref, cz_ref, prev_ref, w_ref, b_ref, lg_ref, lb_ref, cv_ref, st_ref, ubuf, ybuf = refs
    else:
        ca_ref, cb_ref, cz_ref, w_ref, b_ref, lg_ref, lb_ref, cv_ref, st_ref, ubuf, ybuf = refs
    t = pl.program_id(1)
    c = ubuf.shape[1]

    @pl.when(t == 0)
    def _():
        ubuf[0:hp, :] = jnp.zeros((hp, c), F32)
        if has_state:
            ubuf[hp - ks:hp, :] = prev_ref[0]

    @pl.when(t > 0)
    def _():
        ubuf[0:hp, :] = ubuf[tl:tl + hp, :]

    ubuf[hp:hp + tl, :] = ca_ref[...] * jax.nn.sigmoid(cb_ref[...])

    for c0 in range(0, c, lc):
        cols = slice(c0, c0 + lc)
        bias = b_ref[:, cols]
        for r in range(0, tl, rc):
            acc = jnp.broadcast_to(bias, (rc, lc))
            for s in range(SUBLANES):
                taps = [j for j in range(ks + 1) if (hp - ks + j) % SUBLANES == s]
                if not taps:
                    continue
                ext = rc + (SUBLANES if s else 0)
                part = None
                for j in taps:
                    a0 = r + (hp - ks + j) - s
                    term = (ubuf[a0:a0 + ext, cols].reshape(ext // SUBLANES, SUBLANES, lc) * w_ref[j, :, cols]
                            ).reshape(ext, lc)
                    part = term if part is None else part + term
                acc = acc + (part[s:s + rc, :] if s else part)
            ybuf[r:r + rc, cols] = acc

    lg = lg_ref[...]
    lb = lb_ref[...]
    for r in range(0, tl, nr):
        rows = slice(r, r + nr)
        conv = ybuf[rows, :]
        mu = jnp.mean(conv, axis=-1, keepdims=True)
        d = conv - mu
        var = jnp.mean(d * d, axis=-1, keepdims=True)
        y = d * lax.rsqrt(var + EPS) * lg + lb
        cv_ref[rows, :] = (_silu(y) * _silu(cz_ref[rows, :])).astype(cv_ref.dtype)

    @pl.when(t == pl.num_programs(1) - 1)
    def _():
        st_ref[0] = ubuf[hp + tl - ks:hp + tl, :]


def _conv_branch(ca, cb, cz, nb, seq, prev, w, b, lg, lb):
    kk, c = w.shape
    ks = kk - 1
    hp = -(-ks // SUBLANES) * SUBLANES
    tl = _tile(seq, 128, 32)
    rc, lc, nr = tl, LANES, 16
    assert tl % rc == 0 and tl % nr == 0 and tl >= hp and seq >= ks
    nt = seq // tl
    row = lambda bi, ti: bi * nt + ti
    vec_spec = pl.BlockSpec((1, c), lambda bi, ti: (0, 0))
    in_specs = [_col_block(tl, c, ca[1], row), _col_block(tl, c, cb[1], row), _col_block(tl, c, cz[1], row)]
    args = [ca[0], cb[0], cz[0]]
    if prev is not None:
        in_specs.append(pl.BlockSpec((1, ks, c), lambda bi, ti: (bi, 0, 0)))
        args.append(prev)
    in_specs += [pl.BlockSpec((kk, SUBLANES, c), lambda bi, ti: (0, 0, 0)), vec_spec, vec_spec, vec_spec]
    args += [jnp.broadcast_to(w[:, None, :], (kk, SUBLANES, c)), b.reshape(1, c), lg.reshape(1, c), lb.reshape(1, c)]
    return pl.pallas_call(
        functools.partial(_conv_kernel, tl=tl, ks=ks, hp=hp, rc=rc, lc=lc, nr=nr, has_state=prev is not None),
        grid=(nb, nt),
        in_specs=in_specs,
        out_specs=[pl.BlockSpec((tl, c), lambda bi, ti: (bi * nt + ti, 0)),
                   pl.BlockSpec((1, ks, c), lambda bi, ti: (bi, 0, 0))],
        out_shape=[jax.ShapeDtypeStruct((nb * seq, c), BF16), jax.ShapeDtypeStruct((nb, ks, c), F32)],
        scratch_shapes=[pltpu.VMEM((hp + tl, c), F32), pltpu.VMEM((tl, c), F32)],
        compiler_params=_params("parallel", "arbitrary"),
        name="conv_branch",
    )(*args)


def _mlstm_kernel(*refs, L, H, dk, dv, has_state):
    if has_state:
        (q_ref, k_ref, v_ref, g_ref, o_ref, z_ref, ng_ref, c0_ref, n0_ref, m0_ref,
         hm_ref, cs_ref, ns_ref, ms_ref) = refs
    else:
        q_ref, k_ref, v_ref, g_ref, o_ref, z_ref, ng_ref, hm_ref, cs_ref, ns_ref, ms_ref = refs

    @pl.when(pl.program_id(1) == 0)
    def _():
        if has_state:
            cs_ref[...] = c0_ref[...]
            ns_ref[...] = n0_ref[...]
            ms_ref[...] = m0_ref[...]
        else:
            cs_ref[...] = jnp.zeros(cs_ref.shape, F32)
            ns_ref[...] = jnp.zeros(ns_ref.shape, F32)
            ms_ref[...] = jnp.zeros(ms_ref.shape, F32)

    row = lax.broadcasted_iota(jnp.int32, (L, L), 0)
    col = lax.broadcasted_iota(jnp.int32, (L, L), 1)
    eye = row == col
    tril = col <= row
    gates = g_ref[...]
    glane = lax.broadcasted_iota(jnp.int32, gates.shape, 1)
    neg_inf = jnp.float32(-jnp.inf)

    for h in range(H):
        li_col = jnp.sum(jnp.where(glane == h, gates, 0.0), axis=1, keepdims=True)
        f_col = jnp.sum(jnp.where(glane == H + h, gates, 0.0), axis=1, keepdims=True)
        lf_col = jnp.minimum(f_col, 0.0) - jnp.log(1.0 + jnp.exp(-jnp.abs(f_col)))
        lf_row = jnp.sum(jnp.where(eye, lf_col, 0.0), axis=0, keepdims=True)
        li_row = jnp.sum(jnp.where(eye, li_col, 0.0), axis=0, keepdims=True)
        bt_col = jnp.sum(jnp.where(tril, lf_row, 0.0), axis=1, keepdims=True)
        bt_row = jnp.sum(jnp.where(row <= col, lf_col, 0.0), axis=0, keepdims=True)

        m_prev = ms_ref[0, 0:1, h:h + 1]
        c_prev = cs_ref[0, h]
        n_prev = ns_ref[0, h:h + 1, :]

        q = q_ref[:, h * dk:(h + 1) * dk]
        k = k_ref[:, h * dk:(h + 1) * dk] * (dk ** -0.5)
        v = v_ref[:, h * dv:(h + 1) * dv]
        qb = q.astype(BF16)
        kb = k.astype(BF16)
        vb = v.astype(BF16)

        logw = jnp.where(tril, bt_col - bt_row + li_row, neg_inf)
        a_col = bt_col + m_prev
        m_t = jnp.maximum(a_col, jnp.max(logw, axis=1, keepdims=True))
        w_inter = jnp.exp(a_col - m_t)
        qk = lax.dot_general(qb, kb, (((1,), (1,)), ((), ())), preferred_element_type=F32)
        sw = jnp.exp(logw - m_t) * qk
        num = (jnp.dot(sw.astype(BF16), vb, preferred_element_type=F32)
               + jnp.dot(qb, c_prev.astype(BF16), preferred_element_type=F32) * w_inter)
        den = jnp.sum(sw, axis=1, keepdims=True) + w_inter * jnp.sum(q * n_prev, axis=1, keepdims=True)
        denom = jnp.maximum(jnp.abs(den), jnp.exp(-m_t))
        hh = num / denom

        b_last = bt_col[L - 1:L, :]
        gs_col = b_last - bt_col + li_col
        m_new = jnp.maximum(b_last + m_prev, jnp.max(gs_col, axis=0, keepdims=True))
        ws_col = jnp.exp(gs_col - m_new)
        decay = jnp.exp(b_last + m_prev - m_new)
        kw = ws_col * k
        cs_ref[0, h] = decay * c_prev + lax.dot_general(
            kw.astype(BF16), vb, (((0,), (0,)), ((), ())), preferred_element_type=F32)
        ns_ref[0, h:h + 1, :] = decay * n_prev + jnp.sum(kw, axis=0, keepdims=True)
        ms_ref[0, 0:1, h:h + 1] = m_new

        sl = slice(h * dv, (h + 1) * dv)
        hn = _rms(hh, ng_ref[:, sl])
        hm_ref[:, sl] = (hn * jax.nn.sigmoid(o_ref[:, sl]) * _silu(z_ref[:, sl])).astype(hm_ref.dtype)


def _mlstm_branch(mq, mk, mv, mif, mo, mz, nb, seq, state, norm_g, H, dk, dv):
    L = min(seq, CHUNK)
    assert seq % L == 0
    nc = seq // L
    row = lambda bi, ci: bi * nc + ci
    ops = ((mq, H * dk), (mk, H * dk), (mv, H * dv), (mif, LANES), (mo, H * dv), (mz, H * dv))
    in_specs = [_col_block(L, width, op[1], row) for op, width in ops]
    in_specs.append(pl.BlockSpec((1, H * dv), lambda bi, ci: (0, 0)))
    args = [op[0] for op, _ in ops] + [norm_g.reshape(1, H * dv)]
    st_specs = [pl.BlockSpec((1, H, dk, dv), lambda bi, ci: (bi, 0, 0, 0)),
                pl.BlockSpec((1, H, dk), lambda bi, ci: (bi, 0, 0)),
                pl.BlockSpec((1, 1, H), lambda bi, ci: (bi, 0, 0))]
    if state is not None:
        c0, n0, m0 = state
        in_specs += st_specs
        args += [c0, n0, m0.reshape(nb, 1, H)]
    hm, cs, ns, ms = pl.pallas_call(
        functools.partial(_mlstm_kernel, L=L, H=H, dk=dk, dv=dv, has_state=state is not None),
        grid=(nb, nc),
        in_specs=in_specs,
        out_specs=[pl.BlockSpec((L, H * dv), lambda bi, ci: (bi * nc + ci, 0))] + st_specs,
        out_shape=[jax.ShapeDtypeStruct((nb * seq, H * dv), BF16),
                   jax.ShapeDtypeStruct((nb, H, dk, dv), F32),
                   jax.ShapeDtypeStruct((nb, H, dk), F32),
                   jax.ShapeDtypeStruct((nb, 1, H), F32)],
        compiler_params=_params("parallel", "arbitrary"),
        name="mlstm_branch",
    )(*args)
    return hm, cs, ns, ms.reshape(nb, H)


def _rope_slot(x, cos, sin, half):
    swapped = pltpu.roll(x, ROPE_SLOT - half, 1) + pltpu.roll(x, half, 1)
    return x * cos + swapped * sin


def _q_kernel(a_ref, cg_ref, w_ref, qn_ref, qr_ref, cos_ref, sin_ref, o_ref, *, hg, nope, rope_dim, scale):
    hq = _rms(a_ref[...], cg_ref[...]).astype(BF16)
    cos = cos_ref[...]
    sin = sin_ref[...]
    slot = nope + ROPE_SLOT
    for h in range(hg):
        b0 = h * slot
        qa = jnp.dot(hq, w_ref[:, b0:b0 + slot], preferred_element_type=F32)
        xn = _rms(qa[:, :nope], qn_ref[...])
        xr = _rope_slot(_rms(qa[:, nope:], qr_ref[...], rope_dim), cos, sin, rope_dim // 2)
        o_ref[:, b0:b0 + nope] = (xn * scale).astype(o_ref.dtype)
        o_ref[:, b0 + nope:b0 + slot] = (xr * scale).astype(o_ref.dtype)


def _q_proj(acq, cq_g, w_uq_slots, qn_g, qr_g_slot, cos, sin, heads, nope, rope_dim, scale):
    m = acq[0].shape[0]
    ql = cq_g.shape[0]
    slot = nope + ROPE_SLOT
    hg = _tile(heads, 4, 1)
    tm = _tile(m, 512, 16)
    vec = lambda n: pl.BlockSpec((1, n), lambda i, j: (0, 0))
    return pl.pallas_call(
        functools.partial(_q_kernel, hg=hg, nope=nope, rope_dim=rope_dim, scale=scale),
        grid=(m // tm, heads // hg),
        in_specs=[_col_block(tm, ql, acq[1], lambda i, j: i), vec(ql),
                  pl.BlockSpec((ql, hg * slot), lambda i, j: (0, j)), vec(nope), vec(ROPE_SLOT),
                  pl.BlockSpec((tm, ROPE_SLOT), lambda i, j: (i, 0)), pl.BlockSpec((tm, ROPE_SLOT), lambda i, j: (i, 0))],
        out_specs=pl.BlockSpec((tm, hg * slot), lambda i, j: (i, j)),
        out_shape=jax.ShapeDtypeStruct((m, heads * slot), BF16),
        compiler_params=_params("parallel", "parallel"),
        name="q_proj",
    )(acq[0], cq_g.reshape(1, ql), w_uq_slots, qn_g.reshape(1, nope), qr_g_slot.reshape(1, ROPE_SLOT), cos, sin)


def _latent_kernel(c_ref, r_ref, cg_ref, rg_ref, cos_ref, sin_ref, lat_ref, kr_ref, *, rope_dim):
    lat_ref[...] = _rms(c_ref[...], cg_ref[...])
    r = r_ref[...]
    r = jnp.where(lax.broadcasted_iota(jnp.int32, r.shape, 1) < rope_dim, r, 0.0)
    kr_ref[...] = _rope_slot(_rms(r, rg_ref[...], rope_dim), cos_ref[...], sin_ref[...], rope_dim // 2)


def _latent_norm(ackv, akr, ckv_g, kr_g_slot, cos, sin, rope_dim):
    m = ackv[0].shape[0]
    kvl = ckv_g.shape[0]
    tm = _tile(m, 1024, 8)
    rows = lambda n: pl.BlockSpec((tm, n), lambda i: (i, 0))
    return pl.pallas_call(
        functools.partial(_latent_kernel, rope_dim=rope_dim),
        grid=(m // tm,),
        in_specs=[_col_block(tm, kvl, ackv[1], lambda i: i), _col_block(tm, ROPE_SLOT, akr[1], lambda i: i),
                  pl.BlockSpec((1, kvl), lambda i: (0, 0)), pl.BlockSpec((1, ROPE_SLOT), lambda i: (0, 0)),
                  rows(ROPE_SLOT), rows(ROPE_SLOT)],
        out_specs=[rows(kvl), rows(ROPE_SLOT)],
        out_shape=[jax.ShapeDtypeStruct((m, kvl), F32), jax.ShapeDtypeStruct((m, ROPE_SLOT), F32)],
        compiler_params=_params("parallel"),
        name="latent_norm",
    )(ackv[0], akr[0], ckv_g.reshape(1, kvl), kr_g_slot.reshape(1, ROPE_SLOT), cos, sin)


def _kv_kernel(lat_ref, kr_ref, w_ref, kn_ref, kc_ref, v_ref, *, hg, nope, vd):
    lat = lat_ref[...].astype(BF16)
    kr = kr_ref[...].astype(kc_ref.dtype)
    slot = nope + ROPE_SLOT
    for h in range(hg):
        b0 = h * (nope + vd)
        kv = jnp.dot(lat, w_ref[:, b0:b0 + nope + vd], preferred_element_type=F32)
        kc_ref[:, h * slot:h * slot + nope] = _rms(kv[:, :nope], kn_ref[...]).astype(kc_ref.dtype)
        kc_ref[:, h * slot + nope:(h + 1) * slot] = kr
        v_ref[:, h * vd:(h + 1) * vd] = kv[:, nope:].astype(v_ref.dtype)


def _kv_up(lat, kr_slot, w_ukv, kn_g, heads, nope, vd, row0=0, rows=None):
    kvl = lat.shape[1]
    rows = lat.shape[0] if rows is None else rows
    slot = nope + ROPE_SLOT
    hg = _tile(heads, 4, 1)
    tm = _tile(rows, 1024, 16)
    assert row0 % tm == 0
    i0 = row0 // tm
    return pl.pallas_call(
        functools.partial(_kv_kernel, hg=hg, nope=nope, vd=vd),
        grid=(rows // tm, heads // hg),
        in_specs=[pl.BlockSpec((tm, kvl), lambda i, j: (i0 + i, 0)), pl.BlockSpec((tm, ROPE_SLOT), lambda i, j: (i0 + i, 0)),
                  pl.BlockSpec((kvl, hg * (nope + vd)), lambda i, j: (0, j)), pl.BlockSpec((1, nope), lambda i, j: (0, 0))],
        out_specs=[pl.BlockSpec((tm, hg * slot), lambda i, j: (i, j)), pl.BlockSpec((tm, hg * vd), lambda i, j: (i, j))],
        out_shape=[jax.ShapeDtypeStruct((rows, heads * slot), BF16), jax.ShapeDtypeStruct((rows, heads * vd), BF16)],
        compiler_params=_params("parallel", "parallel"),
        name="kv_up",
    )(lat, kr_slot, w_ukv, kn_g.reshape(1, nope))


def _scores(q, k):
    return lax.dot_general(q, k, (((1,), (1,)), ((), ())), preferred_element_type=F32)


def _softmax_step(s, v, m, l, acc, mask):
    if mask is not None:
        s = jnp.where(mask, s, -jnp.inf)
    m_new = jnp.maximum(m, jnp.max(s, axis=1, keepdims=True))
    alpha = jnp.exp2(m - m_new)
    p = jnp.exp2(s - m_new)
    l = alpha * l + jnp.sum(p, axis=1, keepdims=True)
    acc = alpha * acc + jnp.dot(p.astype(v.dtype), v, preferred_element_type=F32)
    return m_new, l, acc


def _softmax_init(rows, vd):
    return (jnp.full((rows, 1), -jnp.inf, F32), jnp.zeros((rows, 1), F32), jnp.zeros((rows, vd), F32))


def _attn_prefill_kernel(q_ref, k_ref, v_ref, z_ref, o_ref, *, tq, vd):
    i = pl.program_id(2)
    q = q_ref[...]
    ktile = lambda j: k_ref[pl.ds(pl.multiple_of(j * tq, tq), tq), :]
    vtile = lambda j: v_ref[pl.ds(pl.multiple_of(j * tq, tq), tq), :]

    def body(j, carry):
        state, s_cur = carry
        s_next = _scores(q, ktile(j + 1))
        return _softmax_step(s_cur, vtile(j), *state, None), s_next

    state, s_diag = lax.fori_loop(0, i, body, (_softmax_init(tq, vd), _scores(q, ktile(0))))
    rchunk = lax.broadcasted_iota(jnp.int32, (tq, tq), 0) // CHUNK
    cchunk = lax.broadcasted_iota(jnp.int32, (tq, tq), 1) // CHUNK
    _, l, acc = _softmax_step(s_diag, vtile(i), *state, cchunk <= rchunk)
    o_ref[...] = (acc / l * _silu(z_ref[...].astype(F32))).astype(o_ref.dtype)


def _attn_prefill(qc, kc, vv, az, nb, seq, heads, nope, vd):
    slot = nope + ROPE_SLOT
    tq = _tile(seq, 512, CHUNK)
    nq = seq // tq
    return pl.pallas_call(
        functools.partial(_attn_prefill_kernel, tq=tq, vd=vd),
        grid=(nb, heads, nq),
        in_specs=[pl.BlockSpec((tq, slot), lambda b, h, i: (b * nq + i, h)),
                  pl.BlockSpec((seq, slot), lambda b, h, i: (b, h)),
                  pl.BlockSpec((seq, vd), lambda b, h, i: (b, h)),
                  pl.BlockSpec((tq, vd), lambda b, h, i: (b * nq + i, az[1] // vd + h))],
        out_specs=pl.BlockSpec((tq, vd), lambda b, h, i: (b * nq + i, h)),
        out_shape=jax.ShapeDtypeStruct((nb * seq, heads * vd), BF16),
        compiler_params=_params("parallel", "parallel", "arbitrary"),
        name="attn_prefill",
    )(qc, kc, vv, az[0])


def _attn_decode_kernel(q_ref, kp_ref, vp_ref, kn_ref, vn_ref, z_ref, o_ref, *, past, vd, slot, hg):
    lq = q_ref.shape[0]

    def mask(t, base):
        qchunk = (past + lax.broadcasted_iota(jnp.int32, (lq, t), 0)) // CHUNK
        kchunk = (base + lax.broadcasted_iota(jnp.int32, (lq, t), 1)) // CHUNK
        return kchunk <= qchunk

    mask_past = mask(past, 0)
    mask_new = mask(lq, past)
    for g in range(hg):
        q = q_ref[:, g * slot:(g + 1) * slot]
        ks = slice(g * slot, (g + 1) * slot)
        vs = slice(g * vd, (g + 1) * vd)
        carry = _softmax_step(_scores(q, kp_ref[:, ks]), vp_ref[:, vs], *_softmax_init(lq, vd), mask_past)
        _, l, acc = _softmax_step(_scores(q, kn_ref[:, ks]), vn_ref[:, vs], *carry, mask_new)
        o_ref[:, vs] = (acc / l * _silu(z_ref[:, vs].astype(F32))).astype(o_ref.dtype)


def _attn_decode(qc, kc_past, v_past, kc_new, v_new, az, nb, seq, past, heads, nope, vd):
    slot = nope + ROPE_SLOT
    hg = _tile(heads, 4, 1)
    assert past > 0
    return pl.pallas_call(
        functools.partial(_attn_decode_kernel, past=past, vd=vd, slot=slot, hg=hg),
        grid=(nb, heads // hg),
        in_specs=[pl.BlockSpec((seq, hg * slot), lambda b, h: (b, h)),
                  pl.BlockSpec((past, hg * slot), lambda b, h: (b, h)),
                  pl.BlockSpec((past, hg * vd), lambda b, h: (b, h)),
                  pl.BlockSpec((seq, hg * slot), lambda b, h: (b, h)),
                  pl.BlockSpec((seq, hg * vd), lambda b, h: (b, h)),
                  pl.BlockSpec((seq, hg * vd), lambda b, h: (b, az[1] // (hg * vd) + h))],
        out_specs=pl.BlockSpec((seq, hg * vd), lambda b, h: (b, h)),
        out_shape=jax.ShapeDtypeStruct((nb * seq, heads * vd), BF16),
        compiler_params=_params("parallel", "parallel"),
        name="attn_decode",
    )(qc, kc_past, v_past, kc_new, v_new, az[0])


def _merge_kernel(cv_ref, hm_ref, az_ref, wc_ref, wm_ref, wa_ref, gc_ref, gm_ref, ga_ref, o_ref):
    yc = jnp.dot(cv_ref[...], wc_ref[...], preferred_element_type=F32)
    ym = jnp.dot(hm_ref[...], wm_ref[...], preferred_element_type=F32)
    ya = jnp.dot(az_ref[...], wa_ref[...], preferred_element_type=F32)
    y = (jax.nn.sigmoid(gc_ref[...].astype(F32)) * yc + jax.nn.sigmoid(gm_ref[...].astype(F32)) * ym
         + jax.nn.sigmoid(ga_ref[...].astype(F32)) * ya)
    o_ref[...] = y.astype(o_ref.dtype)


def _merge(cv, hm, az, w_pc, w_pm, w_pa, gates, d):
    m = cv.shape[0]
    tm = _tile(m, 1024, 16)
    tn = _tile(d, 1024, LANES)
    nj = d // tn
    assert gates[1] % tn == 0
    g0 = gates[1] // tn
    rows = lambda a: pl.BlockSpec((tm, a.shape[1]), lambda i, j: (i, 0))
    wcol = lambda w: pl.BlockSpec((w.shape[0], tn), lambda i, j: (0, j))
    gate = lambda b: pl.BlockSpec((tm, tn), lambda i, j, b=b: (i, g0 + b * nj + j))
    return pl.pallas_call(
        _merge_kernel,
        grid=(m // tm, nj),
        in_specs=[rows(cv), rows(hm), rows(az), wcol(w_pc), wcol(w_pm), wcol(w_pa), gate(0), gate(1), gate(2)],
        out_specs=pl.BlockSpec((tm, tn), lambda i, j: (i, j)),
        out_shape=jax.ShapeDtypeStruct((m, d), BF16),
        compiler_params=_params("parallel", "parallel"),
        name="merge",
    )(cv, hm, az, w_pc, w_pm, w_pa, gates[0], gates[0], gates[0])


def _pad_cols(a, width):
    return jnp.pad(a, [(0, 0)] * (a.ndim - 1) + [(0, width - a.shape[-1])])


def _rope_tables(pos, half):
    freqs = ROPE_THETA ** (-jnp.arange(half, dtype=F32) / half)
    ang = pos.astype(F32)[:, None] * freqs
    cos, sin = jnp.cos(ang), jnp.sin(ang)
    return (_pad_cols(jnp.concatenate([cos, cos], axis=-1), ROPE_SLOT),
            _pad_cols(jnp.concatenate([-sin, sin], axis=-1), ROPE_SLOT))


def kernel(x_prompt, x_sample, cache_kv_latent, cache_k_rope, state_conv, state_mlstm_C, state_mlstm_n, state_mlstm_m, ln_g, w_in, b_in, conv_w, conv_b, conv_ln_g, conv_ln_b, w_pc, m_norm_g, w_pm, cq_g, ckv_g, qn_g, qr_g, kn_g, kr_g, w_uq, w_ukv, w_pa, w_out):
    bp, lp, d = x_prompt.shape
    bs, ls, _ = x_sample.shape
    depth = w_in.shape[0]
    past = cache_kv_latent.shape[2]
    kvl = cache_kv_latent.shape[3]
    rope_dim = cache_k_rope.shape[3]
    half = rope_dim // 2
    cw = conv_w.shape[2]
    H, dk, dv = state_mlstm_C.shape[2:]
    mw = H * dv
    ql = cq_g.shape[1]
    nope = qn_g.shape[1]
    heads = w_uq.shape[2] // (nope + rope_dim)
    vd = w_ukv.shape[2] // heads - nope
    aw = heads * vd
    qscale = (nope + rope_dim) ** -0.5 * math.log2(math.e)
    assert rope_dim % 2 == 0 and rope_dim <= ROPE_SLOT and 2 * H <= LANES

    sizes = (cw, cw, cw, H * dk, H * dk, mw, H, H, mw, mw, ql, kvl, rope_dim, aw, N_BRANCH * d)
    offs = [0]
    for s in sizes:
        offs.append(offs[-1] + s)
    o_mi, o_mo, o_ckv, o_az = offs[6], offs[8], offs[11], offs[13]
    assert offs[-1] == w_in.shape[2]
    colA = dict(ca=0, cb=cw, cz=2 * cw, mq=3 * cw, mk=3 * cw + H * dk, mv=3 * cw + 2 * H * dk, gif=o_mi)
    colB = dict(mo=0, mz=mw, cq=2 * mw, ckv=o_ckv - o_mo, kr=o_ckv - o_mo + kvl)
    n_a = o_mi + MXU_WIDTH
    n_b = -(-(colB["kr"] + ROPE_SLOT) // MXU_WIDTH) * MXU_WIDTH
    n_c = w_in.shape[2] - o_az
    assert o_mo + n_b <= w_in.shape[2] and n_c % MXU_WIDTH == 0

    groups = (
        dict(name="prompt", x=x_prompt.reshape(bp * lp, d), nb=bp, seq=lp, past=0,
             pos=jnp.tile(jnp.arange(lp, dtype=jnp.int32), bp)),
        dict(name="sample", x=x_sample.reshape(bs * ls, d), nb=bs, seq=ls, past=past,
             pos=jnp.tile(past + jnp.arange(ls, dtype=jnp.int32), bs)),
    )
    for g in groups:
        g["cos"], g["sin"] = _rope_tables(g["pos"], half)
        g["outs"] = [[] for _ in range(6)]

    cache_lat_rows = cache_kv_latent.reshape(depth * bs * past, kvl)
    cache_kr_rows = _pad_cols(cache_k_rope, ROPE_SLOT).reshape(depth * bs * past, ROPE_SLOT)
    n_in = w_in.shape[2]
    w_in_rows = jnp.swapaxes(w_in, 1, 2).reshape(depth * n_in, d)
    for l in range(depth):
        w_a = _cast_rows(w_in_rows, l * n_in, n_a)
        w_b = _cast_rows(w_in_rows, l * n_in + o_mo, n_b)
        w_c = _cast_rows(w_in_rows, l * n_in + o_az, n_c)
        b_a, b_b, b_c = b_in[l, :n_a], b_in[l, o_mo:o_mo + n_b], b_in[l, o_az:]
        w_uq_slots = _pad_cols(w_uq[l].reshape(ql, heads, nope + rope_dim), nope + ROPE_SLOT
                               ).reshape(ql, heads * (nope + ROPE_SLOT)).astype(BF16)
        qr_g_slot = _pad_cols(qr_g[l], ROPE_SLOT)
        kr_g_slot = _pad_cols(kr_g[l], ROPE_SLOT)
        w_ukv_b = w_ukv[l].astype(BF16)
        w_pc_b, w_pm_b, w_pa_b, w_out_b = (w[l].astype(BF16) for w in (w_pc, w_pm, w_pa, w_out))

        for g in groups:
            nb, seq = g["nb"], g["seq"]
            prompt = g["past"] == 0
            h = _rmsnorm_rows(g["x"], ln_g[l])
            pa = _matmul(h, w_a, bias=b_a, w_rows=(0, n_a), tn=_tile(n_a, 1024, MXU_WIDTH), name="in_proj_a")
            pb = _matmul(h, w_b, bias=b_b, w_rows=(0, n_b), tn=_tile(n_b, 1024, MXU_WIDTH), name="in_proj_b")
            pc = _matmul(h, w_c, bias=b_c, w_rows=(0, n_c), tn=_tile(n_c, 1024, MXU_WIDTH), out_dtype=BF16,
                         name="in_proj_c")

            cv, cst = _conv_branch((pa, colA["ca"]), (pa, colA["cb"]), (pa, colA["cz"]), nb, seq,
                                   None if prompt else state_conv[l], conv_w[l], conv_b[l], conv_ln_g[l], conv_ln_b[l])
            state = None if prompt else (state_mlstm_C[l], state_mlstm_n[l], state_mlstm_m[l])
            hm, c_new, n_new, m_new = _mlstm_branch(
                (pa, colA["mq"]), (pa, colA["mk"]), (pa, colA["mv"]), (pa, colA["gif"]), (pb, colB["mo"]), (pb, colB["mz"]),
                nb, seq, state, m_norm_g[l], H, dk, dv)

            qc = _q_proj((pb, colB["cq"]), cq_g[l], w_uq_slots, qn_g[l], qr_g_slot, g["cos"], g["sin"],
                         heads, nope, rope_dim, qscale)
            lat, kr_slot = _latent_norm((pb, colB["ckv"]), (pb, colB["kr"]), ckv_g[l], kr_g_slot, g["cos"], g["sin"], rope_dim)
            kc, vv = _kv_up(lat, kr_slot, w_ukv_b, kn_g[l], heads, nope, vd)
            if prompt:
                az = _attn_prefill(qc, kc, vv, (pc, 0), nb, seq, heads, nope, vd)
            else:
                kc_past, v_past = _kv_up(cache_lat_rows, cache_kr_rows, w_ukv_b, kn_g[l], heads, nope, vd,
                                         row0=l * nb * past, rows=nb * past)
                az = _attn_decode(qc, kc_past, v_past, kc, vv, (pc, 0), nb, seq, past, heads, nope, vd)

            merged = _merge(cv, hm, az, w_pc_b, w_pm_b, w_pa_b, (pc, aw), d)
            g["x"] = _matmul(merged, w_out_b, res=g["x"], name="out_proj")
            for lst, val in zip(g["outs"], (cst, c_new, n_new, m_new, lat.reshape(nb, seq, kvl),
                                            kr_slot[:, :rope_dim].reshape(nb, seq, rope_dim))):
                lst.append(val)

    gp, gs = groups
    st = lambda g, k: jnp.stack(g["outs"][k])
    return (gp["x"].reshape(bp, lp, d), gs["x"].reshape(bs, ls, d),
            st(gp, 0), st(gs, 0),
            st(gp, 1), st(gp, 2), st(gp, 3),
            st(gs, 1), st(gs, 2), st(gs, 3),
            st(gp, 4), st(gp, 5), st(gs, 4), st(gs, 5))
```

```python
import functools
import math

import jax
import jax.numpy as jnp
from jax import lax
from jax.experimental import pallas as pl
from jax.experimental.pallas import tpu as pltpu

CHUNK = 64
EPS = 1e-6
ROPE_THETA = 10000.0
N_BRANCH = 3

LANES = 128
SUBLANES = 8
MXU_WIDTH = 256
VMEM_LIMIT_BYTES = 56 * 1024 * 1024
ROPE_SLOT = LANES

BF16 = jnp.bfloat16
F32 = jnp.float32


def _tile(n, pref, align):
    t = min(pref, n)
    t -= t % align
    while t >= align:
        if n % t == 0:
            return t
        t -= align
    return n


def _params(*sem):
    return pltpu.CompilerParams(dimension_semantics=sem, vmem_limit_bytes=VMEM_LIMIT_BYTES)


def _rms(x, g, n=None):
    n = x.shape[-1] if n is None else n
    ms = jnp.sum(x * x, axis=-1, keepdims=True) * (1.0 / n)
    return x * lax.rsqrt(ms + EPS) * g


def _silu(x):
    return x * jax.nn.sigmoid(x)


def _col_block(rows, width, off, row_fn):
    assert off % width == 0, (off, width)
    return pl.BlockSpec((rows, width), lambda *g: (row_fn(*g), off // width))


def _rms_kernel(x_ref, g_ref, o_ref):
    o_ref[...] = _rms(x_ref[...], g_ref[...]).astype(o_ref.dtype)


def _rmsnorm_rows(x, g, tm=512):
    m, d = x.shape
    tm = _tile(m, tm, 16)
    return pl.pallas_call(
        _rms_kernel,
        grid=(m // tm,),
        in_specs=[pl.BlockSpec((tm, d), lambda i: (i, 0)), pl.BlockSpec((1, d), lambda i: (0, 0))],
        out_specs=pl.BlockSpec((tm, d), lambda i: (i, 0)),
        out_shape=jax.ShapeDtypeStruct((m, d), BF16),
        compiler_params=_params("parallel"),
        name="rmsnorm",
    )(x, g.reshape(1, d))


def _cast_kernel(w_ref, o_ref):
    o_ref[...] = w_ref[...].astype(o_ref.dtype)


def _cast_rows(w, row0, n, tr=512):
    k = w.shape[1]
    tr = _tile(n, tr, 16)
    assert row0 % SUBLANES == 0 and w.dtype == F32
    return pl.pallas_call(
        _cast_kernel,
        grid=(n // tr,),
        in_specs=[pl.BlockSpec((pl.Element(tr), pl.Element(k)), lambda i: (pl.multiple_of(row0 + i * tr, SUBLANES), 0))],
        out_specs=pl.BlockSpec((tr, k), lambda i: (i, 0)),
        out_shape=jax.ShapeDtypeStruct((n, k), BF16),
        compiler_params=_params("parallel"),
        name="cast_rows",
    )(w)


def _mm_kernel(*refs, has_bias, has_res, w_rows):
    a_ref, w_ref = refs[0], refs[1]
    o_ref = refs[-1]
    if w_rows:
        acc = lax.dot_general(a_ref[...], w_ref[...].astype(BF16), (((1,), (1,)), ((), ())), preferred_element_type=F32)
    else:
        acc = jnp.dot(a_ref[...], w_ref[...], preferred_element_type=F32)
    k = 2
    if has_bias:
        acc = acc + refs[k][...]
        k += 1
    if has_res:
        acc = acc + refs[k][...]
    o_ref[...] = acc.astype(o_ref.dtype)


def _matmul(a, w, bias=None, res=None, out_dtype=F32, tm=1024, tn=1024, w_rows=None, name="matmul"):
    m, k = a.shape
    tm = _tile(m, tm, 16)
    if w_rows is None:
        n = w.shape[1]
        tn = _tile(n, tn, LANES)
        w_spec = pl.BlockSpec((k, tn), lambda i, j: (0, j))
    else:
        row0, n = w_rows
        tn = _tile(n, tn, LANES)
        row_align = SUBLANES * 4 // w.dtype.itemsize
        assert w.shape[1] == k and row0 % row_align == 0
        w_spec = pl.BlockSpec((pl.Element(tn), pl.Element(k)),
                              lambda i, j: (pl.multiple_of(row0 + j * tn, row_align), 0))
    in_specs = [pl.BlockSpec((tm, k), lambda i, j: (i, 0)), w_spec]
    args = [a, w]
    if bias is not None:
        in_specs.append(pl.BlockSpec((1, tn), lambda i, j: (0, j)))
        args.append(bias.reshape(1, n))
    if res is not None:
        in_specs.append(pl.BlockSpec((tm, tn), lambda i, j: (i, j)))
        args.append(res)
    return pl.pallas_call(
        functools.partial(_mm_kernel, has_bias=bias is not None, has_res=res is not None, w_rows=w_rows is not None),
        grid=(m // tm, n // tn),
        in_specs=in_specs,
        out_specs=pl.BlockSpec((tm, tn), lambda i, j: (i, j)),
        out_shape=jax.ShapeDtypeStruct((m, n), out_dtype),
        compiler_params=_params("parallel", "parallel"),
        name=name,
    )(*args)


def _conv_kernel(*refs, tl, ks, hp, rc, lc, nr, has_state):
    if has_state:
        ca_ref, cb_ref, cz_ref, prev_ref, w_ref, b_ref, lg_ref, lb_ref, cv_ref, st_ref, ubuf, ybuf = refs
    else:
        ca_ref, cb_ref, cz_ref, w_ref, b_ref, lg_ref, lb_ref, cv_ref, st_ref, ubuf, ybuf = refs
    t = pl.program_id(1)
    c = ubuf.shape[1]

    @pl.when(t == 0)
    def _():
        ubuf[0:hp, :] = jnp.zeros((hp, c), F32)
        if has_state:
            ubuf[hp - ks:hp, :] = prev_ref[0]

    @pl.when(t > 0)
    def _():
        ubuf[0:hp, :] = ubuf[tl:tl + hp, :]

    ubuf[hp:hp + tl, :] = ca_ref[...] * jax.nn.sigmoid(cb_ref[...])

    for c0 in range(0, c, lc):
        cols = slice(c0, c0 + lc)
        bias = b_ref[:, cols]
        for r in range(0, tl, rc):
            acc = jnp.broadcast_to(bias, (rc, lc))
            for s in range(SUBLANES):
                taps = [j for j in range(ks + 1) if (hp - ks + j) % SUBLANES == s]
                if not taps:
                    continue
                ext = rc + (SUBLANES if s else 0)
                part = None
                for j in taps:
                    a0 = r + (hp - ks + j) - s
                    term = (ubuf[a0:a0 + ext, cols].reshape(ext // SUBLANES, SUBLANES, lc) * w_ref[j, :, cols]
                            ).reshape(ext, lc)
                    part = term if part is None else part + term
                acc = acc + (part[s:s + rc, :] if s else part)
            ybuf[r:r + rc, cols] = acc

    lg = lg_ref[...]
    lb = lb_ref[...]
    for r in range(0, tl, nr):
        rows = slice(r, r + nr)
        conv = ybuf[rows, :]
        mu = jnp.mean(conv, axis=-1, keepdims=True)
        d = conv - mu
        var = jnp.mean(d * d, axis=-1, keepdims=True)
        y = d * lax.rsqrt(var + EPS) * lg + lb
        cv_ref[rows, :] = (_silu(y) * _silu(cz_ref[rows, :])).astype(cv_ref.dtype)

    @pl.when(t == pl.num_programs(1) - 1)
    def _():
        st_ref[0] = ubuf[hp + tl - ks:hp + tl, :]


def _conv_branch(ca, cb, cz, nb, seq, prev, w, b, lg, lb):
    kk, c = w.shape
    ks = kk - 1
    hp = -(-ks // SUBLANES) * SUBLANES
    tl = _tile(seq, 128, 32)
    rc, lc, nr = tl, LANES, 16
    assert tl % rc == 0 and tl % nr == 0 and tl >= hp and seq >= ks
    nt = seq // tl
    row = lambda bi, ti: bi * nt + ti
    vec_spec = pl.BlockSpec((1, c), lambda bi, ti: (0, 0))
    in_specs = [_col_block(tl, c, ca[1], row), _col_block(tl, c, cb[1], row), _col_block(tl, c, cz[1], row)]
    args = [ca[0], cb[0], cz[0]]
    if prev is not None:
        in_specs.append(pl.BlockSpec((1, ks, c), lambda bi, ti: (bi, 0, 0)))
        args.append(prev)
    in_specs += [pl.BlockSpec((kk, SUBLANES, c), lambda bi, ti: (0, 0, 0)), vec_spec, vec_spec, vec_spec]
    args += [jnp.broadcast_to(w[:, None, :], (kk, SUBLANES, c)), b.reshape(1, c), lg.reshape(1, c), lb.reshape(1, c)]
    return pl.pallas_call(
        functools.partial(_conv_kernel, tl=tl, ks=ks, hp=hp, rc=rc, lc=lc, nr=nr, has_state=prev is not None),
        grid=(nb, nt),
        in_specs=in_specs,
        out_specs=[pl.BlockSpec((tl, c), lambda bi, ti: (bi * nt + ti, 0)),
                   pl.BlockSpec((1, ks, c), lambda bi, ti: (bi, 0, 0))],
        out_shape=[jax.ShapeDtypeStruct((nb * seq, c), BF16), jax.ShapeDtypeStruct((nb, ks, c), F32)],
        scratch_shapes=[pltpu.VMEM((hp + tl, c), F32), pltpu.VMEM((tl, c), F32)],
        compiler_params=_params("parallel", "arbitrary"),
        name="conv_branch",
    )(*args)


def _mlstm_kernel(*refs, L, H, dk, dv, has_state):
    if has_state:
        (q_ref, k_ref, v_ref, g_ref, o_ref, z_ref, ng_ref, c0_ref, n0_ref, m0_ref,
         hm_ref, cs_ref, ns_ref, ms_ref) = refs
    else:
        q_ref, k_ref, v_ref, g_ref, o_ref, z_ref, ng_ref, hm_ref, cs_ref, ns_ref, ms_ref = refs

    @pl.when(pl.program_id(1) == 0)
    def _():
        if has_state:
            cs_ref[...] = c0_ref[...]
            ns_ref[...] = n0_ref[...]
            ms_ref[...] = m0_ref[...]
        else:
            cs_ref[...] = jnp.zeros(cs_ref.shape, F32)
            ns_ref[...] = jnp.zeros(ns_ref.shape, F32)
            ms_ref[...] = jnp.zeros(ms_ref.shape, F32)

    heads = range(H)
    blk = lambda x, h: x[h * L:(h + 1) * L]
    stack = lambda parts: jnp.concatenate(list(parts), axis=0)
    row = lax.broadcasted_iota(jnp.int32, (L, L), 0)
    col = lax.broadcasted_iota(jnp.int32, (L, L), 1)
    eye = row == col
    tril_s = (lax.broadcasted_iota(jnp.int32, (H * L, L), 1)
              <= lax.rem(lax.broadcasted_iota(jnp.int32, (H * L, L), 0), L))
    gates = g_ref[...]
    glane = lax.broadcasted_iota(jnp.int32, gates.shape, 1)
    neg_inf = jnp.float32(-jnp.inf)

    m_prev = [ms_ref[0, 0:1, h:h + 1] for h in heads]
    c_prev = [cs_ref[0, h] for h in heads]
    n_prev = [ns_ref[0, h:h + 1, :] for h in heads]
    q = [q_ref[:, h * dk:(h + 1) * dk] for h in heads]
    k = [k_ref[:, h * dk:(h + 1) * dk] * (dk ** -0.5) for h in heads]
    qb = [x.astype(BF16) for x in q]
    kb = [x.astype(BF16) for x in k]
    vb = [v_ref[:, h * dv:(h + 1) * dv].astype(BF16) for h in heads]
    qk = stack(lax.dot_general(qb[h], kb[h], (((1,), (1,)), ((), ())), preferred_element_type=F32) for h in heads)
    qc = stack(jnp.dot(qb[h], c_prev[h].astype(BF16), preferred_element_type=F32) for h in heads)

    li_col = stack(jnp.sum(jnp.where(glane == h, gates, 0.0), axis=1, keepdims=True) for h in heads)
    f_col = stack(jnp.sum(jnp.where(glane == H + h, gates, 0.0), axis=1, keepdims=True) for h in heads)
    lf_col = jnp.minimum(f_col, 0.0) - jnp.log(1.0 + jnp.exp(-jnp.abs(f_col)))

    def rows_of(colvec, mask):
        return stack(jnp.broadcast_to(jnp.sum(jnp.where(mask, blk(colvec, h), 0.0), axis=0, keepdims=True), (L, L))
                     for h in heads)

    li_row = rows_of(li_col, eye)
    lf_row = rows_of(lf_col, eye)
    bt_row = rows_of(lf_col, row <= col)
    bt_col = jnp.sum(jnp.where(tril_s, lf_row, 0.0), axis=1, keepdims=True)

    logw = jnp.where(tril_s, bt_col - bt_row + li_row, neg_inf)
    a_col = bt_col + stack(jnp.broadcast_to(m_prev[h], (L, 1)) for h in heads)
    m_t = jnp.maximum(a_col, jnp.max(logw, axis=1, keepdims=True))
    w_inter = jnp.exp(a_col - m_t)
    sw = jnp.exp(logw - m_t) * qk
    swb = sw.astype(BF16)
    num = stack(jnp.dot(blk(swb, h), vb[h], preferred_element_type=F32) for h in heads) + qc * w_inter
    qn = jnp.sum(stack(q) * stack(jnp.broadcast_to(n_prev[h], (L, dk)) for h in heads), axis=1, keepdims=True)
    den = jnp.sum(sw, axis=1, keepdims=True) + w_inter * qn
    hh = num / jnp.maximum(jnp.abs(den), jnp.exp(-m_t))
    hn = hh * lax.rsqrt(jnp.mean(hh * hh, axis=-1, keepdims=True) + EPS)
    out_gate = ng_ref[...] * jax.nn.sigmoid(o_ref[...]) * _silu(z_ref[...])
    for h in heads:
        sl = slice(h * dv, (h + 1) * dv)
        hm_ref[:, sl] = (blk(hn, h) * out_gate[:, sl]).astype(hm_ref.dtype)

    b_last = [blk(bt_col, h)[L - 1:L, :] for h in heads]
    gs_col = stack(jnp.broadcast_to(b_last[h], (L, 1)) for h in heads) - bt_col + li_col
    m_new = [jnp.maximum(b_last[h] + m_prev[h], jnp.max(blk(gs_col, h), axis=0, keepdims=True)) for h in heads]
    kw = jnp.exp(gs_col - stack(jnp.broadcast_to(m_new[h], (L, 1)) for h in heads)) * stack(k)
    kwb = kw.astype(BF16)
    for h in heads:
        decay = jnp.exp(b_last[h] + m_prev[h] - m_new[h])
        cs_ref[0, h] = decay * c_prev[h] + lax.dot_general(
            blk(kwb, h), vb[h], (((0,), (0,)), ((), ())), preferred_element_type=F32)
        ns_ref[0, h:h + 1, :] = decay * n_prev[h] + jnp.sum(blk(kw, h), axis=0, keepdims=True)
        ms_ref[0, 0:1, h:h + 1] = m_new[h]


def _mlstm_branch(mq, mk, mv, mif, mo, mz, nb, seq, state, norm_g, H, dk, dv):
    L = min(seq, CHUNK)
    assert seq % L == 0
    nc = seq // L
    row = lambda bi, ci: bi * nc + ci
    ops = ((mq, H * dk), (mk, H * dk), (mv, H * dv), (mif, LANES), (mo, H * dv), (mz, H * dv))
    in_specs = [_col_block(L, width, op[1], row) for op, width in ops]
    in_specs.append(pl.BlockSpec((1, H * dv), lambda bi, ci: (0, 0)))
    args = [op[0] for op, _ in ops] + [norm_g.reshape(1, H * dv)]
    st_specs = [pl.BlockSpec((1, H, dk, dv), lambda bi, ci: (bi, 0, 0, 0)),
                pl.BlockSpec((1, H, dk), lambda bi, ci: (bi, 0, 0)),
                pl.BlockSpec((1, 1, H), lambda bi, ci: (bi, 0, 0))]
    if state is not None:
        c0, n0, m0 = state
        in_specs += st_specs
        args += [c0, n0, m0.reshape(nb, 1, H)]
    hm, cs, ns, ms = pl.pallas_call(
        functools.partial(_mlstm_kernel, L=L, H=H, dk=dk, dv=dv, has_state=state is not None),
        grid=(nb, nc),
        in_specs=in_specs,
        out_specs=[pl.BlockSpec((L, H * dv), lambda bi, ci: (bi * nc + ci, 0))] + st_specs,
        out_shape=[jax.ShapeDtypeStruct((nb * seq, H * dv), BF16),
                   jax.ShapeDtypeStruct((nb, H, dk, dv), F32),
                   jax.ShapeDtypeStruct((nb, H, dk), F32),
                   jax.ShapeDtypeStruct((nb, 1, H), F32)],
        compiler_params=_params("parallel", "arbitrary"),
        name="mlstm_branch",
    )(*args)
    return hm, cs, ns, ms.reshape(nb, H)


def _rope_slot(x, cos, sin, half):
    swapped = pltpu.roll(x, ROPE_SLOT - half, 1) + pltpu.roll(x, half, 1)
    return x * cos + swapped * sin


def _q_kernel(a_ref, cg_ref, w_ref, qn_ref, qr_ref, cos_ref, sin_ref, o_ref, *, hg, nope, rope_dim, scale):
    hq = _rms(a_ref[...], cg_ref[...]).astype(BF16)
    cos = cos_ref[...]
    sin = sin_ref[...]
    slot = nope + ROPE_SLOT
    for h in range(hg):
        b0 = h * slot
        qa = jnp.dot(hq, w_ref[:, b0:b0 + slot], preferred_element_type=F32)
        xn = _rms(qa[:, :nope], qn_ref[...])
        xr = _rope_slot(_rms(qa[:, nope:], qr_ref[...], rope_dim), cos, sin, rope_dim // 2)
        o_ref[:, b0:b0 + nope] = (xn * scale).astype(o_ref.dtype)
        o_ref[:, b0 + nope:b0 + slot] = (xr * scale).astype(o_ref.dtype)


def _q_proj(acq, cq_g, w_uq_slots, qn_g, qr_g_slot, cos, sin, heads, nope, rope_dim, scale):
    m = acq[0].shape[0]
    ql = cq_g.shape[0]
    slot = nope + ROPE_SLOT
    hg = _tile(heads, 4, 1)
    tm = _tile(m, 512, 16)
    vec = lambda n: pl.BlockSpec((1, n), lambda i, j: (0, 0))
    return pl.pallas_call(
        functools.partial(_q_kernel, hg=hg, nope=nope, rope_dim=rope_dim, scale=scale),
        grid=(m // tm, heads // hg),
        in_specs=[_col_block(tm, ql, acq[1], lambda i, j: i), vec(ql),
                  pl.BlockSpec((ql, hg * slot), lambda i, j: (0, j)), vec(nope), vec(ROPE_SLOT),
                  pl.BlockSpec((tm, ROPE_SLOT), lambda i, j: (i, 0)), pl.BlockSpec((tm, ROPE_SLOT), lambda i, j: (i, 0))],
        out_specs=pl.BlockSpec((tm, hg * slot), lambda i, j: (i, j)),
        out_shape=jax.ShapeDtypeStruct((m, heads * slot), BF16),
        compiler_params=_params("parallel", "parallel"),
        name="q_proj",
    )(acq[0], cq_g.reshape(1, ql), w_uq_slots, qn_g.reshape(1, nope), qr_g_slot.reshape(1, ROPE_SLOT), cos, sin)


def _latent_kernel(c_ref, r_ref, cg_ref, rg_ref, cos_ref, sin_ref, lat_ref, kr_ref, *, rope_dim):
    lat_ref[...] = _rms(c_ref[...], cg_ref[...])
    r = r_ref[...]
    r = jnp.where(lax.broadcasted_iota(jnp.int32, r.shape, 1) < rope_dim, r, 0.0)
    kr_ref[...] = _rope_slot(_rms(r, rg_ref[...], rope_dim), cos_ref[...], sin_ref[...], rope_dim // 2)


def _latent_norm(ackv, akr, ckv_g, kr_g_slot, cos, sin, rope_dim):
    m = ackv[0].shape[0]
    kvl = ckv_g.shape[0]
    tm = _tile(m, 1024, 8)
    rows = lambda n: pl.BlockSpec((tm, n), lambda i: (i, 0))
    return pl.pallas_call(
        functools.partial(_latent_kernel, rope_dim=rope_dim),
        grid=(m // tm,),
        in_specs=[_col_block(tm, kvl, ackv[1], lambda i: i), _col_block(tm, ROPE_SLOT, akr[1], lambda i: i),
                  pl.BlockSpec((1, kvl), lambda i: (0, 0)), pl.BlockSpec((1, ROPE_SLOT), lambda i: (0, 0)),
                  rows(ROPE_SLOT), rows(ROPE_SLOT)],
        out_specs=[rows(kvl), rows(ROPE_SLOT)],
        out_shape=[jax.ShapeDtypeStruct((m, kvl), F32), jax.ShapeDtypeStruct((m, ROPE_SLOT), F32)],
        compiler_params=_params("parallel"),
        name="latent_norm",
    )(ackv[0], akr[0], ckv_g.reshape(1, kvl), kr_g_slot.reshape(1, ROPE_SLOT), cos, sin)


def _kv_kernel(lat_ref, kr_ref, w_ref, kn_ref, kc_ref, v_ref, *, hg, nope, vd):
    lat = lat_ref[...].astype(BF16)
    kr = kr_ref[...].astype(kc_ref.dtype)
    slot = nope + ROPE_SLOT
    for h in range(hg):
        b0 = h * (nope + vd)
        kv = jnp.dot(lat, w_ref[:, b0:b0 + nope + vd], preferred_element_type=F32)
        kc_ref[:, h * slot:h * slot + nope] = _rms(kv[:, :nope], kn_ref[...]).astype(kc_ref.dtype)
        kc_ref[:, h * slot + nope:(h + 1) * slot] = kr
        v_ref[:, h * vd:(h + 1) * vd] = kv[:, nope:].astype(v_ref.dtype)


def _kv_up(lat, kr_slot, w_ukv, kn_g, heads, nope, vd, row0=0, rows=None):
    kvl = lat.shape[1]
    rows = lat.shape[0] if rows is None else rows
    slot = nope + ROPE_SLOT
    hg = _tile(heads, 4, 1)
    tm = _tile(rows, 1024, 16)
    assert row0 % tm == 0
    i0 = row0 // tm
    return pl.pallas_call(
        functools.partial(_kv_kernel, hg=hg, nope=nope, vd=vd),
        grid=(rows // tm, heads // hg),
        in_specs=[pl.BlockSpec((tm, kvl), lambda i, j: (i0 + i, 0)), pl.BlockSpec((tm, ROPE_SLOT), lambda i, j: (i0 + i, 0)),
                  pl.BlockSpec((kvl, hg * (nope + vd)), lambda i, j: (0, j)), pl.BlockSpec((1, nope), lambda i, j: (0, 0))],
        out_specs=[pl.BlockSpec((tm, hg * slot), lambda i, j: (i, j)), pl.BlockSpec((tm, hg * vd), lambda i, j: (i, j))],
        out_shape=[jax.ShapeDtypeStruct((rows, heads * slot), BF16), jax.ShapeDtypeStruct((rows, heads * vd), BF16)],
        compiler_params=_params("parallel", "parallel"),
        name="kv_up",
    )(lat, kr_slot, w_ukv, kn_g.reshape(1, nope))


def _scores(q, k):
    return lax.dot_general(q, k, (((1,), (1,)), ((), ())), preferred_element_type=F32)


def _softmax_step(s, v, m, l, acc, mask):
    if mask is not None:
        s = jnp.where(mask, s, -jnp.inf)
    m_new = jnp.maximum(m, jnp.max(s, axis=1, keepdims=True))
    alpha = jnp.exp2(m - m_new)
    p = jnp.exp2(s - m_new)
    l = alpha * l + jnp.sum(p, axis=1, keepdims=True)
    acc = alpha * acc + jnp.dot(p.astype(v.dtype), v, preferred_element_type=F32)
    return m_new, l, acc


def _softmax_init(rows, vd):
    return (jnp.full((rows, 1), -jnp.inf, F32), jnp.zeros((rows, 1), F32), jnp.zeros((rows, vd), F32))


def _attn_prefill_kernel(q_ref, k_ref, v_ref, z_ref, o_ref, *, tq, vd):
    i = pl.program_id(2)
    q = q_ref[...]
    ktile = lambda j: k_ref[pl.ds(pl.multiple_of(j * tq, tq), tq), :]
    vtile = lambda j: v_ref[pl.ds(pl.multiple_of(j * tq, tq), tq), :]

    def body(j, carry):
        state, s_cur = carry
        s_next = _scores(q, ktile(j + 1))
        return _softmax_step(s_cur, vtile(j), *state, None), s_next

    state, s_diag = lax.fori_loop(0, i, body, (_softmax_init(tq, vd), _scores(q, ktile(0))))
    rchunk = lax.broadcasted_iota(jnp.int32, (tq, tq), 0) // CHUNK
    cchunk = lax.broadcasted_iota(jnp.int32, (tq, tq), 1) // CHUNK
    _, l, acc = _softmax_step(s_diag, vtile(i), *state, cchunk <= rchunk)
    o_ref[...] = (acc / l * _silu(z_ref[...].astype(F32))).astype(o_ref.dtype)


def _attn_prefill(qc, kc, vv, az, nb, seq, heads, nope, vd):
    slot = nope + ROPE_SLOT
    tq = _tile(seq, 512, CHUNK)
    nq = seq // tq
    return pl.pallas_call(
        functools.partial(_attn_prefill_kernel, tq=tq, vd=vd),
        grid=(nb, heads, nq),
        in_specs=[pl.BlockSpec((tq, slot), lambda b, h, i: (b * nq + i, h)),
                  pl.BlockSpec((seq, slot), lambda b, h, i: (b, h)),
                  pl.BlockSpec((seq, vd), lambda b, h, i: (b, h)),
                  pl.BlockSpec((tq, vd), lambda b, h, i: (b * nq + i, az[1] // vd + h))],
        out_specs=pl.BlockSpec((tq, vd), lambda b, h, i: (b * nq + i, h)),
        out_shape=jax.ShapeDtypeStruct((nb * seq, heads * vd), BF16),
        compiler_params=_params("parallel", "parallel", "arbitrary"),
        name="attn_prefill",
    )(qc, kc, vv, az[0])


def _attn_decode_kernel(q_ref, kp_ref, vp_ref, kn_ref, vn_ref, z_ref, o_ref, *, past, vd, slot, hg):
    lq = q_ref.shape[0]

    def mask(t, base):
        qchunk = (past + lax.broadcasted_iota(jnp.int32, (lq, t), 0)) // CHUNK
        kchunk = (base + lax.broadcasted_iota(jnp.int32, (lq, t), 1)) // CHUNK
        return kchunk <= qchunk

    mask_past = mask(past, 0)
    mask_new = mask(lq, past)
    for g in range(hg):
        q = q_ref[:, g * slot:(g + 1) * slot]
        ks = slice(g * slot, (g + 1) * slot)
        vs = slice(g * vd, (g + 1) * vd)
        carry = _softmax_step(_scores(q, kp_ref[:, ks]), vp_ref[:, vs], *_softmax_init(lq, vd), mask_past)
        _, l, acc = _softmax_step(_scores(q, kn_ref[:, ks]), vn_ref[:, vs], *carry, mask_new)
        o_ref[:, vs] = (acc / l * _silu(z_ref[:, vs].astype(F32))).astype(o_ref.dtype)


def _attn_decode(qc, kc_past, v_past, kc_new, v_new, az, nb, seq, past, heads, nope, vd):
    slot = nope + ROPE_SLOT
    hg = _tile(heads, 4, 1)
    assert past > 0
    return pl.pallas_call(
        functools.partial(_attn_decode_kernel, past=past, vd=vd, slot=slot, hg=hg),
        grid=(nb, heads // hg),
        in_specs=[pl.BlockSpec((seq, hg * slot), lambda b, h: (b, h)),
                  pl.BlockSpec((past, hg * slot), lambda b, h: (b, h)),
                  pl.BlockSpec((past, hg * vd), lambda b, h: (b, h)),
                  pl.BlockSpec((seq, hg * slot), lambda b, h: (b, h)),
                  pl.BlockSpec((seq, hg * vd), lambda b, h: (b, h)),
                  pl.BlockSpec((seq, hg * vd), lambda b, h: (b, az[1] // (hg * vd) + h))],
        out_specs=pl.BlockSpec((seq, hg * vd), lambda b, h: (b, h)),
        out_shape=jax.ShapeDtypeStruct((nb * seq, heads * vd), BF16),
        compiler_params=_params("parallel", "parallel"),
        name="attn_decode",
    )(qc, kc_past, v_past, kc_new, v_new, az[0])


def _merge_kernel(cv_ref, hm_ref, az_ref, wc_ref, wm_ref, wa_ref, gc_ref, gm_ref, ga_ref, o_ref):
    yc = jnp.dot(cv_ref[...], wc_ref[...], preferred_element_type=F32)
    ym = jnp.dot(hm_ref[...], wm_ref[...], preferred_element_type=F32)
    ya = jnp.dot(az_ref[...], wa_ref[...], preferred_element_type=F32)
    y = (jax.nn.sigmoid(gc_ref[...].astype(F32)) * yc + jax.nn.sigmoid(gm_ref[...].astype(F32)) * ym
         + jax.nn.sigmoid(ga_ref[...].astype(F32)) * ya)
    o_ref[...] = y.astype(o_ref.dtype)


def _merge(cv, hm, az, w_pc, w_pm, w_pa, gates, d):
    m = cv.shape[0]
    tm = _tile(m, 1024, 16)
    tn = _tile(d, 1024, LANES)
    nj = d // tn
    assert gates[1] % tn == 0
    g0 = gates[1] // tn
    rows = lambda a: pl.BlockSpec((tm, a.shape[1]), lambda i, j: (i, 0))
    wcol = lambda w: pl.BlockSpec((w.shape[0], tn), lambda i, j: (0, j))
    gate = lambda b: pl.BlockSpec((tm, tn), lambda i, j, b=b: (i, g0 + b * nj + j))
    return pl.pallas_call(
        _merge_kernel,
        grid=(m // tm, nj),
        in_specs=[rows(cv), rows(hm), rows(az), wcol(w_pc), wcol(w_pm), wcol(w_pa), gate(0), gate(1), gate(2)],
        out_specs=pl.BlockSpec((tm, tn), lambda i, j: (i, j)),
        out_shape=jax.ShapeDtypeStruct((m, d), BF16),
        compiler_params=_params("parallel", "parallel"),
        name="merge",
    )(cv, hm, az, w_pc, w_pm, w_pa, gates[0], gates[0], gates[0])


def _pad_cols(a, width):
    return jnp.pad(a, [(0, 0)] * (a.ndim - 1) + [(0, width - a.shape[-1])])


def _rope_tables(pos, half):
    freqs = ROPE_THETA ** (-jnp.arange(half, dtype=F32) / half)
    ang = pos.astype(F32)[:, None] * freqs
    cos, sin = jnp.cos(ang), jnp.sin(ang)
    return (_pad_cols(jnp.concatenate([cos, cos], axis=-1), ROPE_SLOT),
            _pad_cols(jnp.concatenate([-sin, sin], axis=-1), ROPE_SLOT))


def kernel(x_prompt, x_sample, cache_kv_latent, cache_k_rope, state_conv, state_mlstm_C, state_mlstm_n, state_mlstm_m, ln_g, w_in, b_in, conv_w, conv_b, conv_ln_g, conv_ln_b, w_pc, m_norm_g, w_pm, cq_g, ckv_g, qn_g, qr_g, kn_g, kr_g, w_uq, w_ukv, w_pa, w_out):
    bp, lp, d = x_prompt.shape
    bs, ls, _ = x_sample.shape
    depth = w_in.shape[0]
    past = cache_kv_latent.shape[2]
    kvl = cache_kv_latent.shape[3]
    rope_dim = cache_k_rope.shape[3]
    half = rope_dim // 2
    cw = conv_w.shape[2]
    H, dk, dv = state_mlstm_C.shape[2:]
    mw = H * dv
    ql = cq_g.shape[1]
    nope = qn_g.shape[1]
    heads = w_uq.shape[2] // (nope + rope_dim)
    vd = w_ukv.shape[2] // heads - nope
    aw = heads * vd
    qscale = (nope + rope_dim) ** -0.5 * math.log2(math.e)
    assert rope_dim % 2 == 0 and rope_dim <= ROPE_SLOT and 2 * H <= LANES

    sizes = (cw, cw, cw, H * dk, H * dk, mw, H, H, mw, mw, ql, kvl, rope_dim, aw, N_BRANCH * d)
    offs = [0]
    for s in sizes:
        offs.append(offs[-1] + s)
    o_mi, o_mo, o_ckv, o_az = offs[6], offs[8], offs[11], offs[13]
    assert offs[-1] == w_in.shape[2]
    colA = dict(ca=0, cb=cw, cz=2 * cw, mq=3 * cw, mk=3 * cw + H * dk, mv=3 * cw + 2 * H * dk, gif=o_mi)
    colB = dict(mo=0, mz=mw, cq=2 * mw, ckv=o_ckv - o_mo, kr=o_ckv - o_mo + kvl)
    n_a = o_mi + MXU_WIDTH
    n_b = -(-(colB["kr"] + ROPE_SLOT) // MXU_WIDTH) * MXU_WIDTH
    n_c = w_in.shape[2] - o_az
    assert o_mo + n_b <= w_in.shape[2] and n_c % MXU_WIDTH == 0

    groups = (
        dict(name="prompt", x=x_prompt.reshape(bp * lp, d), nb=bp, seq=lp, past=0,
             pos=jnp.tile(jnp.arange(lp, dtype=jnp.int32), bp)),
        dict(name="sample", x=x_sample.reshape(bs * ls, d), nb=bs, seq=ls, past=past,
             pos=jnp.tile(past + jnp.arange(ls, dtype=jnp.int32), bs)),
    )
    for g in groups:
        g["cos"], g["sin"] = _rope_tables(g["pos"], half)
        g["outs"] = [[] for _ in range(6)]

    cache_lat_rows = cache_kv_latent.reshape(depth * bs * past, kvl)
    cache_kr_rows = _pad_cols(cache_k_rope, ROPE_SLOT).reshape(depth * bs * past, ROPE_SLOT)
    n_in = w_in.shape[2]
    w_in_rows = jnp.swapaxes(w_in, 1, 2).reshape(depth * n_in, d)
    for l in range(depth):
        w_a = _cast_rows(w_in_rows, l * n_in, n_a)
        w_b = _cast_rows(w_in_rows, l * n_in + o_mo, n_b)
        w_c = _cast_rows(w_in_rows, l * n_in + o_az, n_c)
        b_a, b_b, b_c = b_in[l, :n_a], b_in[l, o_mo:o_mo + n_b], b_in[l, o_az:]
        w_uq_slots = _pad_cols(w_uq[l].reshape(ql, heads, nope + rope_dim), nope + ROPE_SLOT
                               ).reshape(ql, heads * (nope + ROPE_SLOT)).astype(BF16)
        qr_g_slot = _pad_cols(qr_g[l], ROPE_SLOT)
        kr_g_slot = _pad_cols(kr_g[l], ROPE_SLOT)
        w_ukv_b = w_ukv[l].astype(BF16)
        w_pc_b, w_pm_b, w_pa_b, w_out_b = (w[l].astype(BF16) for w in (w_pc, w_pm, w_pa, w_out))

        for g in groups:
            nb, seq = g["nb"], g["seq"]
            prompt = g["past"] == 0
            h = _rmsnorm_rows(g["x"], ln_g[l])
            pa = _matmul(h, w_a, bias=b_a, w_rows=(0, n_a), tn=_tile(n_a, 1024, MXU_WIDTH), name="in_proj_a")
            pb = _matmul(h, w_b, bias=b_b, w_rows=(0, n_b), tn=_tile(n_b, 1024, MXU_WIDTH), name="in_proj_b")
            pc = _matmul(h, w_c, bias=b_c, w_rows=(0, n_c), tn=_tile(n_c, 1024, MXU_WIDTH), out_dtype=BF16,
                         name="in_proj_c")

            cv, cst = _conv_branch((pa, colA["ca"]), (pa, colA["cb"]), (pa, colA["cz"]), nb, seq,
                                   None if prompt else state_conv[l], conv_w[l], conv_b[l], conv_ln_g[l], conv_ln_b[l])
            state = None if prompt else (state_mlstm_C[l], state_mlstm_n[l], state_mlstm_m[l])
            hm, c_new, n_new, m_new = _mlstm_branch(
                (pa, colA["mq"]), (pa, colA["mk"]), (pa, colA["mv"]), (pa, colA["gif"]), (pb, colB["mo"]), (pb, colB["mz"]),
                nb, seq, state, m_norm_g[l], H, dk, dv)

            qc = _q_proj((pb, colB["cq"]), cq_g[l], w_uq_slots, qn_g[l], qr_g_slot, g["cos"], g["sin"],
                         heads, nope, rope_dim, qscale)
            lat, kr_slot = _latent_norm((pb, colB["ckv"]), (pb, colB["kr"]), ckv_g[l], kr_g_slot, g["cos"], g["sin"], rope_dim)
            kc, vv = _kv_up(lat, kr_slot, w_ukv_b, kn_g[l], heads, nope, vd)
            if prompt:
                az = _attn_prefill(qc, kc, vv, (pc, 0), nb, seq, heads, nope, vd)
            else:
                kc_past, v_past = _kv_up(cache_lat_rows, cache_kr_rows, w_ukv_b, kn_g[l], heads, nope, vd,
                                         row0=l * nb * past, rows=nb * past)
                az = _attn_decode(qc, kc_past, v_past, kc, vv, (pc, 0), nb, seq, past, heads, nope, vd)

            merged = _merge(cv, hm, az, w_pc_b, w_pm_b, w_pa_b, (pc, aw), d)
            g["x"] = _matmul(merged, w_out_b, res=g["x"], name="out_proj")
            for lst, val in zip(g["outs"], (cst, c_new, n_new, m_new, lat.reshape(nb, seq, kvl),
                                            kr_slot[:, :rope_dim].reshape(nb, seq, rope_dim))):
                lst.append(val)

    gp, gs = groups
    st = lambda g, k: jnp.stack(g["outs"][k])
    return (gp["x"].reshape(bp, lp, d), gs["x"].reshape(bs, ls, d),
            st(gp, 0), st(gs, 0),
            st(gp, 1), st(gp, 2), st(gp, 3),
            st(gs, 1), st(gs, 2), st(gs, 3),
            st(gp, 4), st(gp, 5), st(gs, 4), st(gs, 5))
```

```python
import functools
import math

import jax
import jax.numpy as jnp
from jax import lax
from jax.experimental import pallas as pl
from jax.experimental.pallas import tpu as pltpu

CHUNK = 64
EPS = 1e-6
ROPE_THETA = 10000.0
N_BRANCH = 3

LANES = 128
SUBLANES = 8
MXU_WIDTH = 256
VMEM_LIMIT_BYTES = 56 * 1024 * 1024
ROPE_SLOT = LANES

BF16 = jnp.bfloat16
F32 = jnp.float32


def _tile(n, pref, align):
    t = min(pref, n)
    t -= t % align
    while t >= align:
        if n % t == 0:
            return t
        t -= align
    return n


def _params(*sem):
    return pltpu.CompilerParams(dimension_semantics=sem, vmem_limit_bytes=VMEM_LIMIT_BYTES)


def _rms(x, g, n=None):
    n = x.shape[-1] if n is None else n
    ms = jnp.sum(x * x, axis=-1, keepdims=True) * (1.0 / n)
    return x * lax.rsqrt(ms + EPS) * g


def _silu(x):
    return x * jax.nn.sigmoid(x)


def _col_block(rows, width, off, row_fn):
    assert off % width == 0, (off, width)
    return pl.BlockSpec((rows, width), lambda *g: (row_fn(*g), off // width))


def _rms_kernel(x_ref, g_ref, o_ref):
    o_ref[...] = _rms(x_ref[...], g_ref[...]).astype(o_ref.dtype)


def _rmsnorm_rows(x, g, tm=512):
    m, d = x.shape
    tm = _tile(m, tm, 16)
    return pl.pallas_call(
        _rms_kernel,
        grid=(m // tm,),
        in_specs=[pl.BlockSpec((tm, d), lambda i: (i, 0)), pl.BlockSpec((1, d), lambda i: (0, 0))],
        out_specs=pl.BlockSpec((tm, d), lambda i: (i, 0)),
        out_shape=jax.ShapeDtypeStruct((m, d), BF16),
        compiler_params=_params("parallel"),
        name="rmsnorm",
    )(x, g.reshape(1, d))


def _cast_kernel(w_ref, o_ref):
    o_ref[...] = w_ref[...].astype(o_ref.dtype)


def _cast_rows(w, row0, n, tr=512):
    k = w.shape[1]
    tr = _tile(n, tr, 16)
    assert row0 % SUBLANES == 0 and w.dtype == F32
    return pl.pallas_call(
        _cast_kernel,
        grid=(n // tr,),
        in_specs=[pl.BlockSpec((pl.Element(tr), pl.Element(k)), lambda i: (pl.multiple_of(row0 + i * tr, SUBLANES), 0))],
        out_specs=pl.BlockSpec((tr, k), lambda i: (i, 0)),
        out_shape=jax.ShapeDtypeStruct((n, k), BF16),
        compiler_params=_params("parallel"),
        name="cast_rows",
    )(w)


def _mm_kernel(*refs, has_bias, has_res, w_rows):
    a_ref, w_ref = refs[0], refs[1]
    o_ref = refs[-1]
    if w_rows:
        acc = lax.dot_general(a_ref[...], w_ref[...].astype(BF16), (((1,), (1,)), ((), ())), preferred_element_type=F32)
    else:
        acc = jnp.dot(a_ref[...], w_ref[...], preferred_element_type=F32)
    k = 2
    if has_bias:
        acc = acc + refs[k][...]
        k += 1
    if has_res:
        acc = acc + refs[k][...]
    o_ref[...] = acc.astype(o_ref.dtype)


def _matmul(a, w, bias=None, res=None, out_dtype=F32, tm=1024, tn=1024, w_rows=None, name="matmul"):
    m, k = a.shape
    tm = _tile(m, tm, 16)
    if w_rows is None:
        n = w.shape[1]
        tn = _tile(n, tn, LANES)
        w_spec = pl.BlockSpec((k, tn), lambda i, j: (0, j))
    else:
        row0, n = w_rows
        tn = _tile(n, tn, LANES)
        row_align = SUBLANES * 4 // w.dtype.itemsize
        assert w.shape[1] == k and row0 % row_align == 0
        w_spec = pl.BlockSpec((pl.Element(tn), pl.Element(k)),
                              lambda i, j: (pl.multiple_of(row0 + j * tn, row_align), 0))
    in_specs = [pl.BlockSpec((tm, k), lambda i, j: (i, 0)), w_spec]
    args = [a, w]
    if bias is not None:
        in_specs.append(pl.BlockSpec((1, tn), lambda i, j: (0, j)))
        args.append(bias.reshape(1, n))
    if res is not None:
        in_specs.append(pl.BlockSpec((tm, tn), lambda i, j: (i, j)))
        args.append(res)
    return pl.pallas_call(
        functools.partial(_mm_kernel, has_bias=bias is not None, has_res=res is not None, w_rows=w_rows is not None),
        grid=(m // tm, n // tn),
        in_specs=in_specs,
        out_specs=pl.BlockSpec((tm, tn), lambda i, j: (i, j)),
        out_shape=jax.ShapeDtypeStruct((m, n), out_dtype),
        compiler_params=_params("parallel", "parallel"),
        name=name,
    )(*args)


def _conv_kernel(*refs, tl, ks, hp, rc, lc, nr, has_state):
    if has_state:
        ca_ref, cb_ref, cz_ref, prev_ref, w_ref, b_ref, lg_ref, lb_ref, cv_ref, st_ref, ubuf, ybuf = refs
    else:
        ca_ref, cb_ref, cz_ref, w_ref, b_ref, lg_ref, lb_ref, cv_ref, st_ref, ubuf, ybuf = refs
    t = pl.program_id(1)
    c = ubuf.shape[1]

    @pl.when(t == 0)
    def _():
        ubuf[0:hp, :] = jnp.zeros((hp, c), F32)
        if has_state:
            ubuf[hp - ks:hp, :] = prev_ref[0]

    @pl.when(t > 0)
    def _():
        ubuf[0:hp, :] = ubuf[tl:tl + hp, :]

    ubuf[hp:hp + tl, :] = ca_ref[...] * jax.nn.sigmoid(cb_ref[...])

    for c0 in range(0, c, lc):
        cols = slice(c0, c0 + lc)
        bias = b_ref[:, cols]
        for r in range(0, tl, rc):
            acc = jnp.broadcast_to(bias, (rc, lc))
            for s in range(SUBLANES):
                taps = [j for j in range(ks + 1) if (hp - ks + j) % SUBLANES == s]
                if not taps:
                    continue
                ext = rc + (SUBLANES if s else 0)
                part = None
                for j in taps:
                    a0 = r + (hp - ks + j) - s
                    term = (ubuf[a0:a0 + ext, cols].reshape(ext // SUBLANES, SUBLANES, lc) * w_ref[j, :, cols]
                            ).reshape(ext, lc)
                    part = term if part is None else part + term
                acc = acc + (part[s:s + rc, :] if s else part)
            ybuf[r:r + rc, cols] = acc

    lg = lg_ref[...]
    lb = lb_ref[...]
    for r in range(0, tl, nr):
        rows = slice(r, r + nr)
        conv = ybuf[rows, :]
        mu = jnp.mean(conv, axis=-1, keepdims=True)
        d = conv - mu
        var = jnp.mean(d * d, axis=-1, keepdims=True)
        y = d * lax.rsqrt(var + EPS) * lg + lb
        cv_ref[rows, :] = (_silu(y) * _silu(cz_ref[rows, :])).astype(cv_ref.dtype)

    @pl.when(t == pl.num_programs(1) - 1)
    def _():
        st_ref[0] = ubuf[hp + tl - ks:hp + tl, :]


def _conv_branch(ca, cb, cz, nb, seq, prev, w, b, lg, lb):
    kk, c = w.shape
    ks = kk - 1
    hp = -(-ks // SUBLANES) * SUBLANES
    tl = _tile(seq, 128, 32)
    rc, lc, nr = tl, LANES, 16
    assert tl % rc == 0 and tl % nr == 0 and tl >= hp and seq >= ks
    nt = seq // tl
    row = lambda bi, ti: bi * nt + ti
    vec_spec = pl.BlockSpec((1, c), lambda bi, ti: (0, 0))
    in_specs = [_col_block(tl, c, ca[1], row), _col_block(tl, c, cb[1], row), _col_block(tl, c, cz[1], row)]
    args = [ca[0], cb[0], cz[0]]
    if prev is not None:
        in_specs.append(pl.BlockSpec((1, ks, c), lambda bi, ti: (bi, 0, 0)))
        args.append(prev)
    in_specs += [pl.BlockSpec((kk, SUBLANES, c), lambda bi, ti: (0, 0, 0)), vec_spec, vec_spec, vec_spec]
    args += [jnp.broadcast_to(w[:, None, :], (kk, SUBLANES, c)), b.reshape(1, c), lg.reshape(1, c), lb.reshape(1, c)]
    return pl.pallas_call(
        functools.partial(_conv_kernel, tl=tl, ks=ks, hp=hp, rc=rc, lc=lc, nr=nr, has_state=prev is not None),
        grid=(nb, nt),
        in_specs=in_specs,
        out_specs=[pl.BlockSpec((tl, c), lambda bi, ti: (bi * nt + ti, 0)),
                   pl.BlockSpec((1, ks, c), lambda bi, ti: (bi, 0, 0))],
        out_shape=[jax.ShapeDtypeStruct((nb * seq, c), BF16), jax.ShapeDtypeStruct((nb, ks, c), F32)],
        scratch_shapes=[pltpu.VMEM((hp + tl, c), F32), pltpu.VMEM((tl, c), F32)],
        compiler_params=_params("parallel", "arbitrary"),
        name="conv_branch",
    )(*args)


def _mlstm_kernel(*refs, L, H, dk, dv, has_state):
    if has_state:
        (q_ref, k_ref, v_ref, g_ref, o_ref, z_ref, ng_ref, c0_ref, n0_ref, m0_ref,
         hm_ref, cs_ref, ns_ref, ms_ref) = refs
    else:
        q_ref, k_ref, v_ref, g_ref, o_ref, z_ref, ng_ref, hm_ref, cs_ref, ns_ref, ms_ref = refs

    @pl.when(pl.program_id(1) == 0)
    def _():
        if has_state:
            cs_ref[...] = c0_ref[...]
            ns_ref[...] = n0_ref[...]
            ms_ref[...] = m0_ref[...]
        else:
            cs_ref[...] = jnp.zeros(cs_ref.shape, F32)
            ns_ref[...] = jnp.zeros(ns_ref.shape, F32)
            ms_ref[...] = jnp.zeros(ms_ref.shape, F32)

    heads = range(H)
    blk = lambda x, h: x[h * L:(h + 1) * L]
    stack = lambda parts: jnp.concatenate(list(parts), axis=0)
    row = lax.broadcasted_iota(jnp.int32, (L, L), 0)
    col = lax.broadcasted_iota(jnp.int32, (L, L), 1)
    eye = row == col
    tril_s = (lax.broadcasted_iota(jnp.int32, (H * L, L), 1)
              <= lax.rem(lax.broadcasted_iota(jnp.int32, (H * L, L), 0), L))
    gates = g_ref[...]
    glane = lax.broadcasted_iota(jnp.int32, gates.shape, 1)
    neg_inf = jnp.float32(-jnp.inf)

    m_prev = [ms_ref[0, 0:1, h:h + 1] for h in heads]
    c_prev = [cs_ref[0, h] for h in heads]
    n_prev = [ns_ref[0, h:h + 1, :] for h in heads]
    q = [q_ref[:, h * dk:(h + 1) * dk] for h in heads]
    k = [k_ref[:, h * dk:(h + 1) * dk] * (dk ** -0.5) for h in heads]
    qb = [x.astype(BF16) for x in q]
    kb = [x.astype(BF16) for x in k]
    vb = [v_ref[:, h * dv:(h + 1) * dv].astype(BF16) for h in heads]
    qk = stack(lax.dot_general(qb[h], kb[h], (((1,), (1,)), ((), ())), preferred_element_type=F32) for h in heads)
    qc = stack(jnp.dot(qb[h], c_prev[h].astype(BF16), preferred_element_type=F32) for h in heads)

    li_col = stack(jnp.sum(jnp.where(glane == h, gates, 0.0), axis=1, keepdims=True) for h in heads)
    f_col = stack(jnp.sum(jnp.where(glane == H + h, gates, 0.0), axis=1, keepdims=True) for h in heads)
    lf_col = jnp.minimum(f_col, 0.0) - jnp.log(1.0 + jnp.exp(-jnp.abs(f_col)))

    def rows_of(colvec, mask):
        return stack(jnp.broadcast_to(jnp.sum(jnp.where(mask, blk(colvec, h), 0.0), axis=0, keepdims=True), (L, L))
                     for h in heads)

    li_row = rows_of(li_col, eye)
    lf_row = rows_of(lf_col, eye)
    bt_row = rows_of(lf_col, row <= col)
    bt_col = jnp.sum(jnp.where(tril_s, lf_row, 0.0), axis=1, keepdims=True)

    logw = jnp.where(tril_s, bt_col - bt_row + li_row, neg_inf)
    a_col = bt_col + stack(jnp.broadcast_to(m_prev[h], (L, 1)) for h in heads)
    m_t = jnp.maximum(a_col, jnp.max(logw, axis=1, keepdims=True))
    w_inter = jnp.exp(a_col - m_t)
    sw = jnp.exp(logw - m_t) * qk
    swb = sw.astype(BF16)
    num = stack(jnp.dot(blk(swb, h), vb[h], preferred_element_type=F32) for h in heads) + qc * w_inter
    qn = jnp.sum(stack(q) * stack(jnp.broadcast_to(n_prev[h], (L, dk)) for h in heads), axis=1, keepdims=True)
    den = jnp.sum(sw, axis=1, keepdims=True) + w_inter * qn
    hh = num / jnp.maximum(jnp.abs(den), jnp.exp(-m_t))
    hn = hh * lax.rsqrt(jnp.mean(hh * hh, axis=-1, keepdims=True) + EPS)
    out_gate = ng_ref[...] * jax.nn.sigmoid(o_ref[...]) * _silu(z_ref[...])
    for h in heads:
        sl = slice(h * dv, (h + 1) * dv)
        hm_ref[:, sl] = (blk(hn, h) * out_gate[:, sl]).astype(hm_ref.dtype)

    b_last = [blk(bt_col, h)[L - 1:L, :] for h in heads]
    gs_col = stack(jnp.broadcast_to(b_last[h], (L, 1)) for h in heads) - bt_col + li_col
    m_new = [jnp.maximum(b_last[h] + m_prev[h], jnp.max(blk(gs_col, h), axis=0, keepdims=True)) for h in heads]
    kw = jnp.exp(gs_col - stack(jnp.broadcast_to(m_new[h], (L, 1)) for h in heads)) * stack(k)
    kwb = kw.astype(BF16)
    for h in heads:
        decay = jnp.exp(b_last[h] + m_prev[h] - m_new[h])
        cs_ref[0, h] = decay * c_prev[h] + lax.dot_general(
            blk(kwb, h), vb[h], (((0,), (0,)), ((), ())), preferred_element_type=F32)
        ns_ref[0, h:h + 1, :] = decay * n_prev[h] + jnp.sum(blk(kw, h), axis=0, keepdims=True)
        ms_ref[0, 0:1, h:h + 1] = m_new[h]


def _mlstm_branch(mq, mk, mv, mif, mo, mz, nb, seq, state, norm_g, H, dk, dv):
    L = min(seq, CHUNK)
    assert seq % L == 0
    nc = seq // L
    row = lambda bi, ci: bi * nc + ci
    ops = ((mq, H * dk), (mk, H * dk), (mv, H * dv), (mif, LANES), (mo, H * dv), (mz, H * dv))
    in_specs = [_col_block(L, width, op[1], row) for op, width in ops]
    in_specs.append(pl.BlockSpec((1, H * dv), lambda bi, ci: (0, 0)))
    args = [op[0] for op, _ in ops] + [norm_g.reshape(1, H * dv)]
    st_specs = [pl.BlockSpec((1, H, dk, dv), lambda bi, ci: (bi, 0, 0, 0)),
                pl.BlockSpec((1, H, dk), lambda bi, ci: (bi, 0, 0)),
                pl.BlockSpec((1, 1, H), lambda bi, ci: (bi, 0, 0))]
    if state is not None:
        c0, n0, m0 = state
        in_specs += st_specs
        args += [c0, n0, m0.reshape(nb, 1, H)]
    hm, cs, ns, ms = pl.pallas_call(
        functools.partial(_mlstm_kernel, L=L, H=H, dk=dk, dv=dv, has_state=state is not None),
        grid=(nb, nc),
        in_specs=in_specs,
        out_specs=[pl.BlockSpec((L, H * dv), lambda bi, ci: (bi * nc + ci, 0))] + st_specs,
        out_shape=[jax.ShapeDtypeStruct((nb * seq, H * dv), BF16),
                   jax.ShapeDtypeStruct((nb, H, dk, dv), F32),
                   jax.ShapeDtypeStruct((nb, H, dk), F32),
                   jax.ShapeDtypeStruct((nb, 1, H), F32)],
        compiler_params=_params("parallel", "arbitrary"),
        name="mlstm_branch",
    )(*args)
    return hm, cs, ns, ms.reshape(nb, H)


def _rope_slot(x, cos, sin, half):
    swapped = pltpu.roll(x, ROPE_SLOT - half, 1) + pltpu.roll(x, half, 1)
    return x * cos + swapped * sin


def _q_kernel(a_ref, cg_ref, w_ref, qn_ref, qr_ref, cos_ref, sin_ref, o_ref, *, hg, nope, rope_dim, scale):
    hq = _rms(a_ref[...], cg_ref[...]).astype(BF16)
    cos = cos_ref[...]
    sin = sin_ref[...]
    slot = nope + ROPE_SLOT
    for h in range(hg):
        b0 = h * slot
        qa = jnp.dot(hq, w_ref[:, b0:b0 + slot], preferred_element_type=F32)
        xn = _rms(qa[:, :nope], qn_ref[...])
        xr = _rope_slot(_rms(qa[:, nope:], qr_ref[...], rope_dim), cos, sin, rope_dim // 2)
        o_ref[:, b0:b0 + nope] = (xn * scale).astype(o_ref.dtype)
        o_ref[:, b0 + nope:b0 + slot] = (xr * scale).astype(o_ref.dtype)


def _q_proj(acq, cq_g, w_uq_slots, qn_g, qr_g_slot, cos, sin, heads, nope, rope_dim, scale):
    m = acq[0].shape[0]
    ql = cq_g.shape[0]
    slot = nope + ROPE_SLOT
    hg = _tile(heads, 4, 1)
    tm = _tile(m, 512, 16)
    vec = lambda n: pl.BlockSpec((1, n), lambda i, j: (0, 0))
    return pl.pallas_call(
        functools.partial(_q_kernel, hg=hg, nope=nope, rope_dim=rope_dim, scale=scale),
        grid=(m // tm, heads // hg),
        in_specs=[_col_block(tm, ql, acq[1], lambda i, j: i), vec(ql),
                  pl.BlockSpec((ql, hg * slot), lambda i, j: (0, j)), vec(nope), vec(ROPE_SLOT),
                  pl.BlockSpec((tm, ROPE_SLOT), lambda i, j: (i, 0)), pl.BlockSpec((tm, ROPE_SLOT), lambda i, j: (i, 0))],
        out_specs=pl.BlockSpec((tm, hg * slot), lambda i, j: (i, j)),
        out_shape=jax.ShapeDtypeStruct((m, heads * slot), BF16),
        compiler_params=_params("parallel", "parallel"),
        name="q_proj",
    )(acq[0], cq_g.reshape(1, ql), w_uq_slots, qn_g.reshape(1, nope), qr_g_slot.reshape(1, ROPE_SLOT), cos, sin)


def _latent_kernel(c_ref, r_ref, cg_ref, rg_ref, cos_ref, sin_ref, lat_ref, kr_ref, *, rope_dim):
    lat_ref[...] = _rms(c_ref[...], cg_ref[...])
    r = r_ref[...]
    r = jnp.where(lax.broadcasted_iota(jnp.int32, r.shape, 1) < rope_dim, r, 0.0)
    kr_ref[...] = _rope_slot(_rms(r, rg_ref[...], rope_dim), cos_ref[...], sin_ref[...], rope_dim // 2)


def _latent_norm(ackv, akr, ckv_g, kr_g_slot, cos, sin, rope_dim):
    m = ackv[0].shape[0]
    kvl = ckv_g.shape[0]
    tm = _tile(m, 1024, 8)
    rows = lambda n: pl.BlockSpec((tm, n), lambda i: (i, 0))
    return pl.pallas_call(
        functools.partial(_latent_kernel, rope_dim=rope_dim),
        grid=(m // tm,),
        in_specs=[_col_block(tm, kvl, ackv[1], lambda i: i), _col_block(tm, ROPE_SLOT, akr[1], lambda i: i),
                  pl.BlockSpec((1, kvl), lambda i: (0, 0)), pl.BlockSpec((1, ROPE_SLOT), lambda i: (0, 0)),
                  rows(ROPE_SLOT), rows(ROPE_SLOT)],
        out_specs=[rows(kvl), rows(ROPE_SLOT)],
        out_shape=[jax.ShapeDtypeStruct((m, kvl), F32), jax.ShapeDtypeStruct((m, ROPE_SLOT), F32)],
        compiler_params=_params("parallel"),
        name="latent_norm",
    )(ackv[0], akr[0], ckv_g.reshape(1, kvl), kr_g_slot.reshape(1, ROPE_SLOT), cos, sin)


def _kv_kernel(lat_ref, kr_ref, w_ref, kn_ref, kc_ref, v_ref, *, hg, nope, vd):
    lat = lat_ref[...].astype(BF16)
    kr = kr_ref[...].astype(kc_ref.dtype)
    slot = nope + ROPE_SLOT
    for h in range(hg):
        b0 = h * (nope + vd)
        kv = jnp.dot(lat, w_ref[:, b0:b0 + nope + vd], preferred_element_type=F32)
        kc_ref[:, h * slot:h * slot + nope] = _rms(kv[:, :nope], kn_ref[...]).astype(kc_ref.dtype)
        kc_ref[:, h * slot + nope:(h + 1) * slot] = kr
        v_ref[:, h * vd:(h + 1) * vd] = kv[:, nope:].astype(v_ref.dtype)


def _kv_up(lat, kr_slot, w_ukv, kn_g, heads, nope, vd, row0=0, rows=None):
    kvl = lat.shape[1]
    rows = lat.shape[0] if rows is None else rows
    slot = nope + ROPE_SLOT
    hg = _tile(heads, 8, 1)
    tm = _tile(rows, 1024, 16)
    assert row0 % tm == 0
    i0 = row0 // tm
    return pl.pallas_call(
        functools.partial(_kv_kernel, hg=hg, nope=nope, vd=vd),
        grid=(rows // tm, heads // hg),
        in_specs=[pl.BlockSpec((tm, kvl), lambda i, j: (i0 + i, 0)), pl.BlockSpec((tm, ROPE_SLOT), lambda i, j: (i0 + i, 0)),
                  pl.BlockSpec((kvl, hg * (nope + vd)), lambda i, j: (0, j)), pl.BlockSpec((1, nope), lambda i, j: (0, 0))],
        out_specs=[pl.BlockSpec((tm, hg * slot), lambda i, j: (i, j)), pl.BlockSpec((tm, hg * vd), lambda i, j: (i, j))],
        out_shape=[jax.ShapeDtypeStruct((rows, heads * slot), BF16), jax.ShapeDtypeStruct((rows, heads * vd), BF16)],
        compiler_params=_params("parallel", "parallel"),
        name="kv_up",
    )(lat, kr_slot, w_ukv, kn_g.reshape(1, nope))


def _scores(q, k):
    return lax.dot_general(q, k, (((1,), (1,)), ((), ())), preferred_element_type=F32)


def _softmax_step(s, v, m, l, acc, mask):
    if mask is not None:
        s = jnp.where(mask, s, -jnp.inf)
    m_new = jnp.maximum(m, jnp.max(s, axis=1, keepdims=True))
    alpha = jnp.exp2(m - m_new)
    p = jnp.exp2(s - m_new)
    l = alpha * l + jnp.sum(p, axis=1, keepdims=True)
    acc = alpha * acc + jnp.dot(p.astype(v.dtype), v, preferred_element_type=F32)
    return m_new, l, acc


def _softmax_init(rows, vd):
    return (jnp.full((rows, 1), -jnp.inf, F32), jnp.zeros((rows, 1), F32), jnp.zeros((rows, vd), F32))


def _attn_prefill_kernel(q_ref, k_ref, v_ref, z_ref, o_ref, *, tq, tk, vd):
    i = pl.program_id(2)
    nk = tq // tk
    q = q_ref[...]
    ktile = lambda j: k_ref[pl.ds(pl.multiple_of(j * tk, tk), tk), :]
    vtile = lambda j: v_ref[pl.ds(pl.multiple_of(j * tk, tk), tk), :]

    def body(j, carry):
        state, s_cur = carry
        s_next = _scores(q, ktile(j + 1))
        return _softmax_step(s_cur, vtile(j), *state, None), s_next

    state, s_cur = lax.fori_loop(0, i * nk, body, (_softmax_init(tq, vd), _scores(q, ktile(0))))
    rchunk = lax.broadcasted_iota(jnp.int32, (tq, tk), 0) // CHUNK
    for d in range(nk):
        s_next = _scores(q, ktile(i * nk + d + 1)) if d + 1 < nk else None
        cchunk = (d * tk + lax.broadcasted_iota(jnp.int32, (tq, tk), 1)) // CHUNK
        state = _softmax_step(s_cur, vtile(i * nk + d), *state, cchunk <= rchunk)
        s_cur = s_next
    _, l, acc = state
    o_ref[...] = (acc / l * _silu(z_ref[...].astype(F32))).astype(o_ref.dtype)


def _attn_prefill(qc, kc, vv, az, nb, seq, heads, nope, vd):
    slot = nope + ROPE_SLOT
    tq = _tile(seq, 1024, CHUNK)
    tk = _tile(tq, 512, CHUNK)
    nq = seq // tq
    return pl.pallas_call(
        functools.partial(_attn_prefill_kernel, tq=tq, tk=tk, vd=vd),
        grid=(nb, heads, nq),
        in_specs=[pl.BlockSpec((tq, slot), lambda b, h, i: (b * nq + i, h)),
                  pl.BlockSpec((seq, slot), lambda b, h, i: (b, h)),
                  pl.BlockSpec((seq, vd), lambda b, h, i: (b, h)),
                  pl.BlockSpec((tq, vd), lambda b, h, i: (b * nq + i, az[1] // vd + h))],
        out_specs=pl.BlockSpec((tq, vd), lambda b, h, i: (b * nq + i, h)),
        out_shape=jax.ShapeDtypeStruct((nb * seq, heads * vd), BF16),
        compiler_params=_params("parallel", "parallel", "arbitrary"),
        name="attn_prefill",
    )(qc, kc, vv, az[0])


def _attn_decode_kernel(q_ref, kp_ref, vp_ref, kn_ref, vn_ref, z_ref, o_ref, *, past, vd, slot, hg):
    lq = q_ref.shape[0]

    def mask(t, base):
        qchunk = (past + lax.broadcasted_iota(jnp.int32, (lq, t), 0)) // CHUNK
        kchunk = (base + lax.broadcasted_iota(jnp.int32, (lq, t), 1)) // CHUNK
        return kchunk <= qchunk

    mask_past = mask(past, 0)
    mask_new = mask(lq, past)
    for g in range(hg):
        q = q_ref[:, g * slot:(g + 1) * slot]
        ks = slice(g * slot, (g + 1) * slot)
        vs = slice(g * vd, (g + 1) * vd)
        carry = _softmax_step(_scores(q, kp_ref[:, ks]), vp_ref[:, vs], *_softmax_init(lq, vd), mask_past)
        _, l, acc = _softmax_step(_scores(q, kn_ref[:, ks]), vn_ref[:, vs], *carry, mask_new)
        o_ref[:, vs] = (acc / l * _silu(z_ref[:, vs].astype(F32))).astype(o_ref.dtype)


def _attn_decode(qc, kc_past, v_past, kc_new, v_new, az, nb, seq, past, heads, nope, vd):
    slot = nope + ROPE_SLOT
    hg = _tile(heads, 4, 1)
    assert past > 0
    return pl.pallas_call(
        functools.partial(_attn_decode_kernel, past=past, vd=vd, slot=slot, hg=hg),
        grid=(nb, heads // hg),
        in_specs=[pl.BlockSpec((seq, hg * slot), lambda b, h: (b, h)),
                  pl.BlockSpec((past, hg * slot), lambda b, h: (b, h)),
                  pl.BlockSpec((past, hg * vd), lambda b, h: (b, h)),
                  pl.BlockSpec((seq, hg * slot), lambda b, h: (b, h)),
                  pl.BlockSpec((seq, hg * vd), lambda b, h: (b, h)),
                  pl.BlockSpec((seq, hg * vd), lambda b, h: (b, az[1] // (hg * vd) + h))],
        out_specs=pl.BlockSpec((seq, hg * vd), lambda b, h: (b, h)),
        out_shape=jax.ShapeDtypeStruct((nb * seq, heads * vd), BF16),
        compiler_params=_params("parallel", "parallel"),
        name="attn_decode",
    )(qc, kc_past, v_past, kc_new, v_new, az[0])


def _merge_kernel(cv_ref, hm_ref, az_ref, wc_ref, wm_ref, wa_ref, gc_ref, gm_ref, ga_ref, o_ref):
    yc = jnp.dot(cv_ref[...], wc_ref[...], preferred_element_type=F32)
    ym = jnp.dot(hm_ref[...], wm_ref[...], preferred_element_type=F32)
    ya = jnp.dot(az_ref[...], wa_ref[...], preferred_element_type=F32)
    y = (jax.nn.sigmoid(gc_ref[...].astype(F32)) * yc + jax.nn.sigmoid(gm_ref[...].astype(F32)) * ym
         + jax.nn.sigmoid(ga_ref[...].astype(F32)) * ya)
    o_ref[...] = y.astype(o_ref.dtype)


def _merge(cv, hm, az, w_pc, w_pm, w_pa, gates, d):
    m = cv.shape[0]
    tm = _tile(m, 1024, 16)
    tn = _tile(d, 1024, LANES)
    nj = d // tn
    assert gates[1] % tn == 0
    g0 = gates[1] // tn
    rows = lambda a: pl.BlockSpec((tm, a.shape[1]), lambda i, j: (i, 0))
    wcol = lambda w: pl.BlockSpec((w.shape[0], tn), lambda i, j: (0, j))
    gate = lambda b: pl.BlockSpec((tm, tn), lambda i, j, b=b: (i, g0 + b * nj + j))
    return pl.pallas_call(
        _merge_kernel,
        grid=(m // tm, nj),
        in_specs=[rows(cv), rows(hm), rows(az), wcol(w_pc), wcol(w_pm), wcol(w_pa), gate(0), gate(1), gate(2)],
        out_specs=pl.BlockSpec((tm, tn), lambda i, j: (i, j)),
        out_shape=jax.ShapeDtypeStruct((m, d), BF16),
        compiler_params=_params("parallel", "parallel"),
        name="merge",
    )(cv, hm, az, w_pc, w_pm, w_pa, gates[0], gates[0], gates[0])


def _pad_cols(a, width):
    return jnp.pad(a, [(0, 0)] * (a.ndim - 1) + [(0, width - a.shape[-1])])


def _rope_tables(pos, half):
    freqs = ROPE_THETA ** (-jnp.arange(half, dtype=F32) / half)
    ang = pos.astype(F32)[:, None] * freqs
    cos, sin = jnp.cos(ang), jnp.sin(ang)
    return (_pad_cols(jnp.concatenate([cos, cos], axis=-1), ROPE_SLOT),
            _pad_cols(jnp.concatenate([-sin, sin], axis=-1), ROPE_SLOT))


def kernel(x_prompt, x_sample, cache_kv_latent, cache_k_rope, state_conv, state_mlstm_C, state_mlstm_n, state_mlstm_m, ln_g, w_in, b_in, conv_w, conv_b, conv_ln_g, conv_ln_b, w_pc, m_norm_g, w_pm, cq_g, ckv_g, qn_g, qr_g, kn_g, kr_g, w_uq, w_ukv, w_pa, w_out):
    bp, lp, d = x_prompt.shape
    bs, ls, _ = x_sample.shape
    depth = w_in.shape[0]
    past = cache_kv_latent.shape[2]
    kvl = cache_kv_latent.shape[3]
    rope_dim = cache_k_rope.shape[3]
    half = rope_dim // 2
    cw = conv_w.shape[2]
    H, dk, dv = state_mlstm_C.shape[2:]
    mw = H * dv
    ql = cq_g.shape[1]
    nope = qn_g.shape[1]
    heads = w_uq.shape[2] // (nope + rope_dim)
    vd = w_ukv.shape[2] // heads - nope
    aw = heads * vd
    qscale = (nope + rope_dim) ** -0.5 * math.log2(math.e)
    assert rope_dim % 2 == 0 and rope_dim <= ROPE_SLOT and 2 * H <= LANES

    sizes = (cw, cw, cw, H * dk, H * dk, mw, H, H, mw, mw, ql, kvl, rope_dim, aw, N_BRANCH * d)
    offs = [0]
    for s in sizes:
        offs.append(offs[-1] + s)
    o_mi, o_mo, o_ckv, o_az = offs[6], offs[8], offs[11], offs[13]
    assert offs[-1] == w_in.shape[2]
    colA = dict(ca=0, cb=cw, cz=2 * cw, mq=3 * cw, mk=3 * cw + H * dk, mv=3 * cw + 2 * H * dk, gif=o_mi)
    colB = dict(mo=0, mz=mw, cq=2 * mw, ckv=o_ckv - o_mo, kr=o_ckv - o_mo + kvl)
    n_a = o_mi + MXU_WIDTH
    n_b = -(-(colB["kr"] + ROPE_SLOT) // MXU_WIDTH) * MXU_WIDTH
    n_c = w_in.shape[2] - o_az
    assert o_mo + n_b <= w_in.shape[2] and n_c % MXU_WIDTH == 0

    groups = (
        dict(name="prompt", x=x_prompt.reshape(bp * lp, d), nb=bp, seq=lp, past=0,
             pos=jnp.tile(jnp.arange(lp, dtype=jnp.int32), bp)),
        dict(name="sample", x=x_sample.reshape(bs * ls, d), nb=bs, seq=ls, past=past,
             pos=jnp.tile(past + jnp.arange(ls, dtype=jnp.int32), bs)),
    )
    for g in groups:
        g["cos"], g["sin"] = _rope_tables(g["pos"], half)
        g["outs"] = [[] for _ in range(6)]

    cache_lat_rows = cache_kv_latent.reshape(depth * bs * past, kvl)
    cache_kr_rows = _pad_cols(cache_k_rope, ROPE_SLOT).reshape(depth * bs * past, ROPE_SLOT)
    n_in = w_in.shape[2]
    w_in_rows = jnp.swapaxes(w_in, 1, 2).reshape(depth * n_in, d)
    for l in range(depth):
        w_a = _cast_rows(w_in_rows, l * n_in, n_a)
        w_b = _cast_rows(w_in_rows, l * n_in + o_mo, n_b)
        w_c = _cast_rows(w_in_rows, l * n_in + o_az, n_c)
        b_a, b_b, b_c = b_in[l, :n_a], b_in[l, o_mo:o_mo + n_b], b_in[l, o_az:]
        w_uq_slots = _pad_cols(w_uq[l].reshape(ql, heads, nope + rope_dim), nope + ROPE_SLOT
                               ).reshape(ql, heads * (nope + ROPE_SLOT)).astype(BF16)
        qr_g_slot = _pad_cols(qr_g[l], ROPE_SLOT)
        kr_g_slot = _pad_cols(kr_g[l], ROPE_SLOT)
        w_ukv_b = w_ukv[l].astype(BF16)
        w_pc_b, w_pm_b, w_pa_b, w_out_b = (w[l].astype(BF16) for w in (w_pc, w_pm, w_pa, w_out))

        for g in groups:
            nb, seq = g["nb"], g["seq"]
            prompt = g["past"] == 0
            h = _rmsnorm_rows(g["x"], ln_g[l])
            pa = _matmul(h, w_a, bias=b_a, w_rows=(0, n_a), tn=_tile(n_a, 1024, MXU_WIDTH), name="in_proj_a")
            pb = _matmul(h, w_b, bias=b_b, w_rows=(0, n_b), tn=_tile(n_b, 1024, MXU_WIDTH), name="in_proj_b")
            pc = _matmul(h, w_c, bias=b_c, w_rows=(0, n_c), tn=_tile(n_c, 1024, MXU_WIDTH), out_dtype=BF16,
                         name="in_proj_c")

            cv, cst = _conv_branch((pa, colA["ca"]), (pa, colA["cb"]), (pa, colA["cz"]), nb, seq,
                                   None if prompt else state_conv[l], conv_w[l], conv_b[l], conv_ln_g[l], conv_ln_b[l])
            state = None if prompt else (state_mlstm_C[l], state_mlstm_n[l], state_mlstm_m[l])
            hm, c_new, n_new, m_new = _mlstm_branch(
                (pa, colA["mq"]), (pa, colA["mk"]), (pa, colA["mv"]), (pa, colA["gif"]), (pb, colB["mo"]), (pb, colB["mz"]),
                nb, seq, state, m_norm_g[l], H, dk, dv)

            qc = _q_proj((pb, colB["cq"]), cq_g[l], w_uq_slots, qn_g[l], qr_g_slot, g["cos"], g["sin"],
                         heads, nope, rope_dim, qscale)
            lat, kr_slot = _latent_norm((pb, colB["ckv"]), (pb, colB["kr"]), ckv_g[l], kr_g_slot, g["cos"], g["sin"], rope_dim)
            kc, vv = _kv_up(lat, kr_slot, w_ukv_b, kn_g[l], heads, nope, vd)
            if prompt:
                az = _attn_prefill(qc, kc, vv, (pc, 0), nb, seq, heads, nope, vd)
            else:
                kc_past, v_past = _kv_up(cache_lat_rows, cache_kr_rows, w_ukv_b, kn_g[l], heads, nope, vd,
                                         row0=l * nb * past, rows=nb * past)
                az = _attn_decode(qc, kc_past, v_past, kc, vv, (pc, 0), nb, seq, past, heads, nope, vd)

            merged = _merge(cv, hm, az, w_pc_b, w_pm_b, w_pa_b, (pc, aw), d)
            g["x"] = _matmul(merged, w_out_b, res=g["x"], name="out_proj")
            for lst, val in zip(g["outs"], (cst, c_new, n_new, m_new, lat.reshape(nb, seq, kvl),
                                            kr_slot[:, :rope_dim].reshape(nb, seq, rope_dim))):
                lst.append(val)

    gp, gs = groups
    st = lambda g, k: jnp.stack(g["outs"][k])
    return (gp["x"].reshape(bp, lp, d), gs["x"].reshape(bs, ls, d),
            st(gp, 0), st(gs, 0),
            st(gp, 1), st(gp, 2), st(gp, 3),
            st(gs, 1), st(gs, 2), st(gs, 3),
            st(gp, 4), st(gp, 5), st(gs, 4), st(gs, 5))
```
